```python
import math
import jax
import jax.numpy as jnp
from jax import lax
import numpy as np

D_MODEL = 2048
BATCH = 4
SEQ = 2048
DEPTH = 1
DEC_BATCH = 32
DEC_SEQ = 8
PAST_LEN = 8192
PAGE_SIZE = 128

MIX_A = D_MODEL // 2
MIX_B = D_MODEL - MIX_A
DH_A = 64
DV_A = 2 * DH_A
H_A = MIX_A // DV_A
DH_B = 128
H_B = MIX_B // DH_B
G_B = 2
HPG_B = H_B // G_B
CMP_BLOCK = 32
CMP_STRIDE = 16
SEL_BLOCK = 64
SEL_TOP_N = 16
N_LOCAL = 2
WINDOW = 512
Q_BLOCK = 128
NB_W = WINDOW // Q_BLOCK
D_FF = 4 * D_MODEL
ROPE_THETA = 10000.0
EPS = 1e-6
NEG = -1e30
FORCE = 1e4
SPLIT_SIZES = (H_A * 2 * DH_A, H_A * 2 * DH_A, H_A * DV_A, H_B * DH_B) + (G_B * DH_B,) * 6 + (3 * H_B,)
N_IN = sum(SPLIT_SIZES)

kernel_name = "hymba_diff_nsa_macaron_step"


def rms_norm(x, g):
    xf = x.astype(jnp.float32)
    y = xf * lax.rsqrt(jnp.mean(xf * xf, axis=-1, keepdims=True) + EPS)
    return (y * g.astype(jnp.float32)).astype(x.dtype)


def rope(x, pos):
    half = x.shape[-1] // 2
    inv = ROPE_THETA ** (-jnp.arange(half, dtype=jnp.float32) / half)
    ang = pos.astype(jnp.float32)[:, None] * inv[None, :]
    cos = jnp.cos(ang)[None, :, None, :]
    sin = jnp.sin(ang)[None, :, None, :]
    xf = x.astype(jnp.float32)
    x1, x2 = xf[..., :half], xf[..., half:]
    return jnp.concatenate([x1 * cos - x2 * sin, x2 * cos + x1 * sin], axis=-1).astype(x.dtype)


def masked_softmax(s, mask):
    p = jax.nn.softmax(jnp.where(mask, s, NEG), axis=-1)
    return jnp.where(jnp.any(mask, axis=-1, keepdims=True), p, 0.0)


def swiglu(h, w_gate, w_up, w_down):
    return (jax.nn.silu(h @ w_gate) * (h @ w_up)) @ w_down


def gather_pages(pool, page_table):
    g = pool[page_table]
    return g.reshape(g.shape[0], g.shape[1] * g.shape[2], *pool.shape[2:])


def tail_rows(r, n):
    t = r.shape[1]
    if t >= n:
        return r[:, t - n:]
    return jnp.pad(r, ((0, 0), (n - t, 0), (0, 0), (0, 0)))


def diff_attention(q, k, v, qpos, kpos, lam, subln_g, lam_init):
    scale = DH_A ** -0.5
    mask = kpos[None, :] <= qpos[:, None]
    s1 = jnp.einsum("bqhd,bkhd->bhqk", q[..., :DH_A], k[..., :DH_A]).astype(jnp.float32) * scale
    s2 = jnp.einsum("bqhd,bkhd->bhqk", q[..., DH_A:], k[..., DH_A:]).astype(jnp.float32) * scale
    p = masked_softmax(s1, mask) - lam * masked_softmax(s2, mask)
    o = jnp.einsum("bhqk,bkhd->bqhd", p.astype(v.dtype), v)
    return rms_norm(o, subln_g) * (1.0 - lam_init)


def compress(rows, pe, phi):
    n_cmp = (rows.shape[1] - CMP_BLOCK) // CMP_STRIDE + 1
    idx = np.arange(n_cmp)[:, None] * CMP_STRIDE + np.arange(CMP_BLOCK)[None, :]
    blk = rows[:, idx] + pe[None, None, :, None, :]
    return jnp.einsum("bnlgd,lde->bnge", blk, phi)


def sel_overlap(n_cmp, n_sel):
    i = np.arange(n_cmp)[:, None] * CMP_STRIDE
    j = np.arange(n_sel)[None, :] * SEL_BLOCK
    m = (i < j + SEL_BLOCK) & (i + CMP_BLOCK > j)
    return jnp.asarray(m.astype(np.float32))


def nsa_cmp_sel(q, qpos, kc, vc, cmp_end, ks_blk, vs_blk, sel_map, n_top):
    B, Tq = q.shape[:2]
    n_sel = ks_blk.shape[2]
    scale = DH_B ** -0.5
    qg = q.reshape(B, Tq, G_B, HPG_B, DH_B)
    s = jnp.einsum("bqghd,bngd->bqghn", qg, kc).astype(jnp.float32) * scale
    cmask = (cmp_end[None, :] <= qpos[:, None])[None, :, None, None, :]
    p_cmp = masked_softmax(s, cmask)
    o_cmp = jnp.einsum("bqghn,bngd->bqghd", p_cmp.astype(vc.dtype), vc)
    imp = jnp.einsum("bqghn,nm->bqgm", p_cmp, sel_map)
    blk = jnp.arange(n_sel, dtype=jnp.int32)
    cur = qpos // SEL_BLOCK
    rel = cur[:, None] - blk[None, :]
    visible = rel >= 0
    forced = (blk[None, :] == 0) | (visible & (rel < N_LOCAL))
    score = jnp.where(visible[None, :, None, :],
                      jnp.where(forced[None, :, None, :], FORCE + imp, imp), NEG)
    top_s, top_i = lax.top_k(score, n_top)
    blk_ok = top_s > 0.5 * NEG
    b_ix = jnp.arange(B)[:, None, None, None]
    g_ix = jnp.arange(G_B)[None, None, :, None]
    ksel = ks_blk[b_ix, g_ix, top_i]
    vsel = vs_blk[b_ix, g_ix, top_i]
    kpos = top_i[..., None] * SEL_BLOCK + jnp.arange(SEL_BLOCK, dtype=jnp.int32)
    smask = blk_ok[..., None] & (kpos <= qpos[None, :, None, None, None])
    nk = n_top * SEL_BLOCK
    ss = jnp.einsum("bqghd,bqgnsd->bqghns", qg, ksel).astype(jnp.float32) * scale
    p_sel = masked_softmax(ss.reshape(B, Tq, G_B, HPG_B, nk), smask.reshape(B, Tq, G_B, 1, nk))
    o_sel = jnp.einsum("bqghk,bqgkd->bqghd", p_sel.astype(vsel.dtype), vsel.reshape(B, Tq, G_B, nk, DH_B))
    return o_cmp.reshape(B, Tq, H_B, DH_B), o_sel.reshape(B, Tq, H_B, DH_B)


def window_attention(q, k, v, qpos, kpos):
    B, NB, Tq = q.shape[:3]
    qg = q.reshape(B, NB, Tq, G_B, HPG_B, DH_B)
    s = jnp.einsum("bnqghd,bnkgd->bnghqk", qg, k).astype(jnp.float32) * DH_B ** -0.5
    dist = qpos[:, :, None] - kpos[:, None, :]
    mask = (dist >= 0) & (dist <= WINDOW) & (kpos[:, None, :] >= 0)
    p = masked_softmax(s, mask[None, :, None, None])
    o = jnp.einsum("bnghqk,bnkgd->bnqghd", p.astype(v.dtype), v)
    return o.reshape(B, NB, Tq, H_B, DH_B)


def token_mixing(h, pos, past, win_len, w_in, w_out, lam_p, subln_g, cmp_pe, cmp_phi, lam_init):
    B, T, _ = h.shape
    cuts = [int(v) for v in np.cumsum(SPLIT_SIZES)[:-1]]
    q_a, k_a, v_a, q_b, kc, vc, ks, vs, kw, vw, gl = jnp.split(h @ w_in, cuts, axis=-1)
    q_a = rope(q_a.reshape(B, T, 2 * H_A, DH_A), pos).reshape(B, T, H_A, 2 * DH_A)
    k_a = rope(k_a.reshape(B, T, 2 * H_A, DH_A), pos).reshape(B, T, H_A, 2 * DH_A)
    v_a = v_a.reshape(B, T, H_A, DV_A)
    q_b = rope(q_b.reshape(B, T, H_B, DH_B), pos)
    kc = rope(kc.reshape(B, T, G_B, DH_B), pos)
    vc = vc.reshape(B, T, G_B, DH_B)
    ks = rope(ks.reshape(B, T, G_B, DH_B), pos)
    vs = vs.reshape(B, T, G_B, DH_B)
    kw = rope(kw.reshape(B, T, G_B, DH_B), pos)
    vw = vw.reshape(B, T, G_B, DH_B)
    new_rows = (k_a, v_a, kc, vc, ks, vs)
    if past is None:
        full = new_rows
    else:
        full = tuple(jnp.concatenate([p.astype(n.dtype), n], axis=1) for p, n in zip(past[:6], new_rows))
    fk_a, fv_a, fkc, fvc, fks, fvs = full
    Tk = fk_a.shape[1]
    kpos = jnp.arange(Tk, dtype=jnp.int32)
    lp = lam_p.astype(jnp.float32)
    lam = jnp.exp(jnp.sum(lp[0] * lp[1])) - jnp.exp(jnp.sum(lp[2] * lp[3])) + lam_init
    kc_cmp = compress(fkc, cmp_pe[0], cmp_phi[0])
    vc_cmp = compress(fvc, cmp_pe[1], cmp_phi[1])
    n_cmp = kc_cmp.shape[1]
    cmp_end = jnp.asarray(np.arange(n_cmp) * CMP_STRIDE + CMP_BLOCK - 1, jnp.int32)
    n_sel = -(-Tk // SEL_BLOCK)
    pad = n_sel * SEL_BLOCK - Tk

    def to_sel_blocks(r):
        r = jnp.pad(r, ((0, 0), (0, pad), (0, 0), (0, 0)))
        return r.reshape(B, n_sel, SEL_BLOCK, G_B, DH_B).transpose(0, 3, 1, 2, 4)

    ks_blk, vs_blk = to_sel_blocks(fks), to_sel_blocks(fvs)
    sel_map = sel_overlap(n_cmp, n_sel)
    n_top = min(SEL_TOP_N, n_sel)

    def attend(qa, qb, qpos):
        o_a = diff_attention(qa, fk_a, fv_a, qpos, kpos, lam, subln_g, lam_init)
        o_c, o_s = nsa_cmp_sel(qb, qpos, kc_cmp, vc_cmp, cmp_end, ks_blk, vs_blk, sel_map, n_top)
        return o_a, o_c, o_s

    if past is None:
        nb = T // Q_BLOCK

        def to_b(a):
            return a.reshape(B, nb, Q_BLOCK, *a.shape[2:]).swapaxes(0, 1)

        outs = lax.map(lambda xs: attend(*xs), (to_b(q_a), to_b(q_b), pos.reshape(nb, Q_BLOCK)))
        o_a, o_c, o_s = [o.swapaxes(0, 1).reshape(B, T, *o.shape[3:]) for o in outs]
        band = np.arange(nb)[:, None] + np.arange(NB_W + 1)[None, :]

        def to_band(r):
            rp = jnp.pad(r, ((0, 0), (NB_W * Q_BLOCK, 0), (0, 0), (0, 0)))
            rp = rp.reshape(B, nb + NB_W, Q_BLOCK, G_B, DH_B)
            return rp[:, band].reshape(B, nb, (NB_W + 1) * Q_BLOCK, G_B, DH_B)

        kpos_w = ((band - NB_W)[:, :, None] * Q_BLOCK + np.arange(Q_BLOCK)).reshape(nb, -1)
        o_w = window_attention(q_b.reshape(B, nb, Q_BLOCK, H_B, DH_B), to_band(kw), to_band(vw),
                               pos.reshape(nb, Q_BLOCK), jnp.asarray(kpos_w, jnp.int32))
        o_w = o_w.reshape(B, T, H_B, DH_B)
        new_wk, new_wv = tail_rows(kw, win_len), tail_rows(vw, win_len)
    else:
        o_a, o_c, o_s = attend(q_a, q_b, pos)
        wk = jnp.concatenate([past[6].astype(kw.dtype), kw], axis=1)
        wv = jnp.concatenate([past[7].astype(vw.dtype), vw], axis=1)
        tw = wk.shape[1]
        kpos_w = Tk - tw + jnp.arange(tw, dtype=jnp.int32)
        o_w = window_attention(q_b[:, None], wk[:, None], wv[:, None], pos[None], kpos_w[None])[:, 0]
        new_wk, new_wv = wk[:, tw - win_len:], wv[:, tw - win_len:]
    g = jax.nn.sigmoid(gl.astype(jnp.float32)).reshape(B, T, H_B, 3).astype(h.dtype)
    o_b = g[..., 0:1] * o_c + g[..., 1:2] * o_s + g[..., 2:3] * o_w
    out = jnp.concatenate([o_a.reshape(B, T, MIX_A), o_b.reshape(B, T, MIX_B)], axis=-1) @ w_out
    return out, new_rows + (new_wk, new_wv)


def decoder_layer(x, c, pos, past, win_len, w_mod, b_mod, norm_g, w_in, w_out, lam_p, subln_g,
                  cmp_pe, cmp_phi, ffn_gate, ffn_up, ffn_down, lam_init):
    B, _, D = x.shape
    mod = (jax.nn.silu(c) @ w_mod + b_mod).reshape(B, 3, 3, 1, D)

    def pre(y, i):
        return rms_norm(y, norm_g[2 * i]) * (1.0 + mod[:, i, 1]) + mod[:, i, 0]

    h = pre(x, 0)
    x = x + 0.5 * mod[:, 0, 2] * rms_norm(swiglu(h, ffn_gate[0], ffn_up[0], ffn_down[0]), norm_g[1])
    h = pre(x, 1)
    m, state = token_mixing(h, pos, past, win_len, w_in, w_out, lam_p, subln_g, cmp_pe, cmp_phi, lam_init)
    x = x + mod[:, 1, 2] * rms_norm(m, norm_g[3])
    h = pre(x, 2)
    x = x + 0.5 * mod[:, 2, 2] * rms_norm(swiglu(h, ffn_gate[1], ffn_up[1], ffn_down[1]), norm_g[5])
    return x, state


def setup_inputs(seed: int = 0) -> dict:
    key = jax.random.key(seed)
    ks = jax.random.split(key, 26)
    n_pages = PAST_LEN // PAGE_SIZE
    used = DEC_BATCH * n_pages
    n_pool = used + max(1, used // 4)
    win_len = min(WINDOW, PAST_LEN)
    D = D_MODEL

    def nrm(k, shape, s=1.0):
        return s * jax.random.normal(k, shape, jnp.float32)

    page_table = jax.random.permutation(ks[0], n_pool)[:used].reshape(DEC_BATCH, n_pages).astype(jnp.int32)
    return {
        "x_prompt": nrm(ks[1], (BATCH, SEQ, D)),
        "x_sample": nrm(ks[2], (DEC_BATCH, DEC_SEQ, D)),
        "cache_diff_k": nrm(ks[3], (DEPTH, n_pool, PAGE_SIZE, H_A, 2 * DH_A)),
        "cache_diff_v": nrm(ks[4], (DEPTH, n_pool, PAGE_SIZE, H_A, DV_A)),
        "cache_cmp_k": nrm(ks[5], (DEPTH, n_pool, PAGE_SIZE, G_B, DH_B)),
        "cache_cmp_v": nrm(ks[6], (DEPTH, n_pool, PAGE_SIZE, G_B, DH_B)),
        "cache_sel_k": nrm(ks[7], (DEPTH, n_pool, PAGE_SIZE, G_B, DH_B)),
        "cache_sel_v": nrm(ks[8], (DEPTH, n_pool, PAGE_SIZE, G_B, DH_B)),
        "cache_win_k": nrm(ks[9], (DEPTH, DEC_BATCH, win_len, G_B, DH_B)),
        "cache_win_v": nrm(ks[10], (DEPTH, DEC_BATCH, win_len, G_B, DH_B)),
        "page_table": page_table,
        "c_prompt": nrm(ks[11], (BATCH, D)),
        "c_sample": nrm(ks[12], (DEC_BATCH, D)),
        "w_mod": nrm(ks[13], (DEPTH, D, 9 * D), 0.5 * D ** -0.5),
        "b_mod": nrm(ks[14], (DEPTH, 9 * D), 0.02),
        "norm_g": 1.0 + nrm(ks[15], (DEPTH, 6, D), 0.02),
        "w_in": nrm(ks[16], (DEPTH, D, N_IN), D ** -0.5),
        "w_out": nrm(ks[17], (DEPTH, D, D), D ** -0.5),
        "lam_p": nrm(ks[18], (DEPTH, 4, DH_A), 0.1),
        "subln_g": 1.0 + nrm(ks[19], (DEPTH, DV_A), 0.02),
        "cmp_pe": nrm(ks[20], (DEPTH, 2, CMP_BLOCK, DH_B), 0.1),
        "cmp_phi": nrm(ks[21], (DEPTH, 2, CMP_BLOCK, DH_B, DH_B), (CMP_BLOCK * DH_B) ** -0.5),
        "ffn_gate": nrm(ks[22], (DEPTH, 2, D, D_FF), D ** -0.5),
        "ffn_up": nrm(ks[23], (DEPTH, 2, D, D_FF), D ** -0.5),
        "ffn_down": nrm(ks[24], (DEPTH, 2, D_FF, D), D_FF ** -0.5),
    }


def reference(x_prompt, x_sample, cache_diff_k, cache_diff_v, cache_cmp_k, cache_cmp_v,
              cache_sel_k, cache_sel_v, cache_win_k, cache_win_v, page_table, c_prompt, c_sample,
              w_mod, b_mod, norm_g, w_in, w_out, lam_p, subln_g, cmp_pe, cmp_phi,
              ffn_gate, ffn_up, ffn_down):
    past_len = page_table.shape[1] * cache_diff_k.shape[2]
    pos_p = jnp.arange(x_prompt.shape[1], dtype=jnp.int32)
    pos_s = past_len + jnp.arange(x_sample.shape[1], dtype=jnp.int32)
    win_len = cache_win_k.shape[2]
    yp, ys = x_prompt, x_sample
    st_p, st_s = [], []
    for l in range(DEPTH):
        lam_init = 0.8 - 0.6 * math.exp(-0.3 * l)
        w = (w_mod[l], b_mod[l], norm_g[l], w_in[l], w_out[l], lam_p[l], subln_g[l], cmp_pe[l],
             cmp_phi[l], ffn_gate[l], ffn_up[l], ffn_down[l])
        yp, s_p = decoder_layer(yp, c_prompt, pos_p, None, win_len, *w, lam_init)
        st_p.append(s_p)
        past = tuple(gather_pages(cache[l], page_table) for cache in
                     (cache_diff_k, cache_diff_v, cache_cmp_k, cache_cmp_v, cache_sel_k, cache_sel_v))
        past = past + (cache_win_k[l], cache_win_v[l])
        ys, s_s = decoder_layer(ys, c_sample, pos_s, past, win_len, *w, lam_init)
        st_s.append(s_s)
    dk_p, dv_p, ck_p, cv_p, sk_p, sv_p, wk_p, wv_p = [jnp.stack(a) for a in zip(*st_p)]
    dk_s, dv_s, ck_s, cv_s, sk_s, sv_s, wk_s, wv_s = [jnp.stack(a) for a in zip(*st_s)]
    return (yp, ys, dk_p, dv_p, ck_p, cv_p, sk_p, sv_p, wk_p, wv_p,
            dk_s, dv_s, ck_s, cv_s, sk_s, sv_s, wk_s, wv_s)
```

```python
import functools
import math

import numpy as np
import jax
import jax.numpy as jnp
from jax import lax
from jax.experimental import pallas as pl
from jax.experimental.pallas import tpu as pltpu

F32 = jnp.float32
BF16 = jnp.bfloat16

DH_A = 64
H_A = 8
DH_B = 128
H_B = 8
G_B = 2
HPG_B = H_B // G_B
CMP_BLOCK = 32
CMP_STRIDE = 16
SEL_BLOCK = 64
SEL_TOP_N = 16
N_LOCAL = 2
WINDOW = 512
ROPE_THETA = 10000.0
EPS = 1e-6
NEG = -1e30
FORCE = 1e4

LANES = 128
MIB = 1024 * 1024
PAGES_PER_STEP = 8


def _dot(a, b):
    return jnp.dot(a, b, preferred_element_type=F32)


def _dot_nt(a, b):
    return lax.dot_general(a, b, (((1,), (1,)), ((), ())), preferred_element_type=F32)


def _rms(x, g):
    return x * lax.rsqrt(jnp.mean(x * x, axis=-1, keepdims=True) + EPS) * g


def _cparams(sem, vmem_mib):
    return pltpu.CompilerParams(dimension_semantics=sem, vmem_limit_bytes=vmem_mib * MIB)


def _mod_body(c_ref, w_ref, b_ref, o_ref):
    c = c_ref[...]
    a = (c * jax.nn.sigmoid(c)).astype(BF16)
    o_ref[...] = _dot(a, w_ref[...].astype(BF16)) + b_ref[...]


def _adaln_mod(c, w_mod, b_mod):
    R, D = c.shape
    n_out = w_mod.shape[1]
    tn = 1024
    return pl.pallas_call(
        _mod_body,
        grid=(n_out // tn,),
        in_specs=[pl.BlockSpec((R, D), lambda j: (0, 0)),
                  pl.BlockSpec((D, tn), lambda j: (0, j)),
                  pl.BlockSpec((1, tn), lambda j: (0, j))],
        out_specs=pl.BlockSpec((R, tn), lambda j: (0, j)),
        out_shape=jax.ShapeDtypeStruct((R, n_out), F32),
        compiler_params=_cparams(("arbitrary",), 40),
        name="adaln_mod",
    )(c, w_mod, b_mod)


def _mod_spec(mod, k, tm, rows_per_group):
    if mod.ndim == 4:
        return pl.BlockSpec((None, None, 1, mod.shape[-1]),
                            lambda i, *_, k=k: (k, (i * tm) // rows_per_group, 0, 0))
    return pl.BlockSpec((None, tm, mod.shape[-1]), lambda i, *_, k=k: (k, i, 0))


def _ffn_body(x_ref, sh_ref, sc_ref, gt_ref, gpre_ref, gpost_ref, wg_ref, wu_ref, wd_ref, o_ref,
              h_scr, acc_scr, *, n_chunk):
    f = pl.program_id(1)

    @pl.when(f == 0)
    def _():
        h = _rms(x_ref[...], gpre_ref[...]) * (1.0 + sc_ref[...]) + sh_ref[...]
        h_scr[...] = h.astype(BF16)

    h = h_scr[...]
    g = _dot(h, wg_ref[...])
    u = _dot(h, wu_ref[...])
    a = (g * jax.nn.sigmoid(g) * u).astype(BF16)
    d = acc_scr.shape[1]
    cw = d // n_chunk

    @pl.when(f == 0)
    def _():
        for n in range(n_chunk):
            acc_scr[:, n * cw:(n + 1) * cw] = _dot(a, wd_ref[:, n * cw:(n + 1) * cw])

    @pl.when(f > 0)
    def _():
        for n in range(n_chunk):
            acc_scr[:, n * cw:(n + 1) * cw] += _dot(a, wd_ref[:, n * cw:(n + 1) * cw])

    @pl.when(f == pl.num_programs(1) - 1)
    def _():
        o_ref[...] = x_ref[...] + 0.5 * gt_ref[...] * _rms(acc_scr[...], gpost_ref[...])


def _ffn(x, mod, ks, rows_per_group, g_pre, g_post, wg, wu, wd, tm, tf):
    n, d = x.shape
    dff = wg.shape[1]
    row = lambda i, f: (i, 0)
    const = lambda i, f: (0, 0)
    return pl.pallas_call(
        functools.partial(_ffn_body, n_chunk=4),
        grid=(n // tm, dff // tf),
        in_specs=[pl.BlockSpec((tm, d), row),
                  _mod_spec(mod, ks[0], tm, rows_per_group),
                  _mod_spec(mod, ks[1], tm, rows_per_group),
                  _mod_spec(mod, ks[2], tm, rows_per_group),
                  pl.BlockSpec((1, d), const), pl.BlockSpec((1, d), const),
                  pl.BlockSpec((d, tf), lambda i, f: (0, f)),
                  pl.BlockSpec((d, tf), lambda i, f: (0, f)),
                  pl.BlockSpec((tf, d), lambda i, f: (f, 0))],
        out_specs=pl.BlockSpec((tm, d), row),
        out_shape=jax.ShapeDtypeStruct((n, d), F32),
        scratch_shapes=[pltpu.VMEM((tm, d), BF16), pltpu.VMEM((tm, d), F32)],
        compiler_params=_cparams(("parallel", "arbitrary"), 56),
        name="ffn",
    )(x, mod, mod, mod, g_pre, g_post, wg, wu, wd)


IN_BLOCK = 1024
N_IN_BLOCKS = 6


def _inproj_body(x_ref, sh_ref, sc_ref, g_ref, w_ref, ca_ref, sa_ref, cb_ref, sb_ref,
                 qa_ref, ka_ref, kab_ref, va_ref, vab_ref, qb_ref, kc_ref, vc_ref, ks_ref, vs_ref,
                 kw_ref, vw_ref, nkv_ref, gate_ref, h_scr):
    j = pl.program_id(1)

    @pl.when(j == 0)
    def _():
        h = _rms(x_ref[...], g_ref[...]) * (1.0 + sc_ref[...]) + sh_ref[...]
        h_scr[...] = h.astype(BF16)

    lane = lax.broadcasted_iota(jnp.int32, (1, LANES), 1)
    low32 = (lane & (DH_A - 1)) < (DH_A // 2)

    def cols(c):
        y = _dot(h_scr[...], w_ref[:, c * 256:(c + 1) * 256])
        return y[:, :LANES], y[:, LANES:]

    def rope64(y):
        partner = jnp.where(low32, pltpu.roll(y, LANES - DH_A // 2, 1), pltpu.roll(y, DH_A // 2, 1))
        return y * ca_ref[...] + partner * sa_ref[...]

    def rope128(y):
        return y * cb_ref[...] + pltpu.roll(y, DH_B // 2, 1) * sb_ref[...]

    def tile(t):
        return slice(t * LANES, (t + 1) * LANES)

    @pl.when(j == 0)
    def _():
        for c in range(4):
            for t, y in enumerate(cols(c)):
                qa_ref[:, tile(2 * c + t)] = (rope64(y) * (DH_A ** -0.5)).astype(BF16)

    @pl.when(j == 1)
    def _():
        for c in range(4):
            for t, y in enumerate(cols(c)):
                r = rope64(y)
                ka_ref[:, tile(2 * c + t)] = r
                kab_ref[:, tile(2 * c + t)] = r.astype(BF16)

    @pl.when(j == 2)
    def _():
        for c in range(4):
            for t, y in enumerate(cols(c)):
                va_ref[:, tile(2 * c + t)] = y
                vab_ref[:, tile(2 * c + t)] = y.astype(BF16)

    @pl.when(j == 3)
    def _():
        for c in range(4):
            for t, y in enumerate(cols(c)):
                qb_ref[:, tile(2 * c + t)] = (rope128(y) * (DH_B ** -0.5)).astype(BF16)

    @pl.when(j == 4)
    def _():
        for c, (ref, roped) in enumerate(((kc_ref, True), (vc_ref, False), (ks_ref, True), (vs_ref, False))):
            for t, y in enumerate(cols(c)):
                r = rope128(y) if roped else y
                ref[:, tile(t)] = r
                if c >= 2:
                    nkv_ref[:, tile(2 * (c - 2) + t)] = r.astype(BF16)

    @pl.when(j == 5)
    def _():
        for c, (ref, roped) in enumerate(((kw_ref, True), (vw_ref, False))):
            for t, y in enumerate(cols(c)):
                r = rope128(y) if roped else y
                ref[:, tile(t)] = r
                nkv_ref[:, tile(4 + 2 * c + t)] = r.astype(BF16)
        gate_ref[...] = jax.nn.sigmoid(cols(2)[0])


def _inproj(x, mod, k_shift, k_scale, rows_per_group, g, w_pad, tabs, tm):
    n, d = x.shape
    gq = G_B * DH_B
    n_tab = tabs[0].shape[0] // tm
    row = lambda i, j: (i, 0)
    const = lambda i, j: (0, 0)
    tab = pl.BlockSpec((tm, LANES), lambda i, j: (i % n_tab, 0))
    wide = lambda dt: jax.ShapeDtypeStruct((n, IN_BLOCK), dt)
    narrow = jax.ShapeDtypeStruct((n, gq), F32)
    out_shape = [wide(BF16), wide(F32), wide(BF16), wide(F32), wide(BF16), wide(BF16)] + [narrow] * 6 + [
        wide(BF16), jax.ShapeDtypeStruct((n, LANES), F32)]
    out_specs = [pl.BlockSpec((tm, s.shape[1]), row) for s in out_shape]
    return pl.pallas_call(
        _inproj_body,
        grid=(n // tm, N_IN_BLOCKS),
        in_specs=[pl.BlockSpec((tm, d), row),
                  _mod_spec(mod, k_shift, tm, rows_per_group),
                  _mod_spec(mod, k_scale, tm, rows_per_group),
                  pl.BlockSpec((1, d), const),
                  pl.BlockSpec((d, IN_BLOCK), lambda i, j: (0, j)),
                  tab, tab, tab, tab],
        out_specs=out_specs,
        out_shape=out_shape,
        scratch_shapes=[pltpu.VMEM((tm, d), BF16)],
        compiler_params=_cparams(("parallel", "arbitrary"), 56),
        name="inproj",
    )(x, mod, mod, g, w_pad, *tabs)


def _rope_tables(pos):
    pos = np.asarray(pos, np.float64)[:, None]

    def tables(head_dim):
        half = head_dim // 2
        inv = ROPE_THETA ** (-np.arange(half, dtype=np.float64) / half)
        ang = pos * inv[None, :]
        cos = np.concatenate([np.cos(ang), np.cos(ang)], axis=1)
        sin = np.concatenate([-np.sin(ang), np.sin(ang)], axis=1)
        rep = LANES // head_dim
        return (jnp.asarray(np.tile(cos, (1, rep)), F32), jnp.asarray(np.tile(sin, (1, rep)), F32))

    ca, sa = tables(DH_A)
    cb, sb = tables(DH_B)
    return ca, sa, cb, sb


def _lambda(lamp_ref, lam_init):
    lp = lamp_ref[...]
    a = jnp.sum(lp[0:1] * lp[1:2], axis=1, keepdims=True)
    b = jnp.sum(lp[2:3] * lp[3:4], axis=1, keepdims=True)
    return jnp.exp(a) - jnp.exp(b) + lam_init


def _online_update(s, v, m, l, acc):
    m_new = jnp.maximum(m, jnp.max(s, axis=-1, keepdims=True))
    p = jnp.exp(s - m_new)
    alpha = jnp.exp(m - m_new)
    l = alpha * l + jnp.sum(p, axis=-1, keepdims=True)
    acc = alpha * acc + _dot(p.astype(BF16), v)
    return m_new, l, acc


def _softmax_init(rows, dv):
    return (jnp.full((rows, 1), NEG, F32), jnp.zeros((rows, 1), F32), jnp.zeros((rows, dv), F32))


def _masked_softmax(s, mask):
    sm = jnp.where(mask, s, NEG)
    m = jnp.max(sm, axis=-1, keepdims=True)
    e = jnp.where(mask, jnp.exp(sm - m), 0.0)
    l = jnp.sum(e, axis=-1, keepdims=True)
    return e / jnp.where(l > 0.0, l, 1.0)


def _select_blocks(imp, qpos, n_blocks, n_top):
    shape = imp.shape
    blk = lax.broadcasted_iota(jnp.int32, shape, 1)
    rel = lax.shift_right_logical(qpos, int(math.log2(SEL_BLOCK))) - blk
    visible = rel >= 0
    forced = (blk == 0) | (visible & (rel < N_LOCAL))
    score = jnp.where(visible, jnp.where(forced, FORCE + imp, imp), NEG)
    rank = jnp.zeros(shape, F32)
    for i in range(n_blocks):
        col = score[:, i:i + 1]
        beats = (col > score) | ((col == score) & (blk > i))
        rank = rank + jnp.where(beats, 1.0, 0.0)
    return jnp.where((rank < n_top) & visible, 1.0, 0.0)


def _compressed_kv(y):
    n = y.shape[0]
    return (y[:, :DH_B] + pltpu.roll(y[:, DH_B:], n - 1, 0)).astype(BF16)


def _diff_prompt_body(lamp_ref, g_ref, q_ref, k_ref, v_ref, o_ref, *, tq, lam_init):
    qi = pl.program_id(2)
    lam = _lambda(lamp_ref, lam_init)
    q = q_ref[...]
    lane = lax.broadcasted_iota(jnp.int32, (1, 2 * DH_A), 1)
    zero = jnp.zeros_like(q)
    qq = jnp.concatenate([jnp.where(lane < DH_A, q, zero), jnp.where(lane >= DH_A, q, zero)], axis=0)

    def chunk(j):
        start = pl.multiple_of(j * tq, tq)
        return k_ref[pl.ds(start, tq), :], v_ref[pl.ds(start, tq), :]

    def body(j, carry):
        k, v = chunk(j)
        return _online_update(_dot_nt(qq, k), v, *carry)

    carry = lax.fori_loop(0, qi, body, _softmax_init(2 * tq, 2 * DH_A))
    k, v = chunk(qi)
    s = _dot_nt(qq, k)
    r = lax.broadcasted_iota(jnp.int32, s.shape, 0) & (tq - 1)
    c = lax.broadcasted_iota(jnp.int32, s.shape, 1)
    m, l, acc = _online_update(jnp.where(c <= r, s, NEG), v, *carry)
    o = acc / l
    o_ref[...] = (_rms(o[:tq] - lam * o[tq:], g_ref[...]) * (1.0 - lam_init)).astype(o_ref.dtype)


def _diff_prompt(q, k, v, lam_p, subln_g, batch, seq, lam_init, tq=512):
    n, w = q.shape
    dv = 2 * DH_A
    nq = seq // tq
    return pl.pallas_call(
        functools.partial(_diff_prompt_body, tq=tq, lam_init=lam_init),
        grid=(batch, H_A, nq),
        in_specs=[pl.BlockSpec(lam_p.shape, lambda b, h, i: (0, 0)),
                  pl.BlockSpec((1, dv), lambda b, h, i: (0, 0)),
                  pl.BlockSpec((tq, dv), lambda b, h, i: (b * nq + i, h)),
                  pl.BlockSpec((seq, dv), lambda b, h, i: (b, h)),
                  pl.BlockSpec((seq, dv), lambda b, h, i: (b, h))],
        out_specs=pl.BlockSpec((tq, dv), lambda b, h, i: (b * nq + i, h)),
        out_shape=jax.ShapeDtypeStruct((n, w), BF16),
        compiler_params=_cparams(("parallel", "parallel", "arbitrary"), 48),
        name="diff_prompt",
    )(lam_p, subln_g, q, k, v)


def _cmp_const_body(pek_ref, pev_ref, wk_ref, wv_ref, o_ref):
    for a, (pe_ref, w_ref) in enumerate(((pek_ref, wk_ref), (pev_ref, wv_ref))):
        cw = _dot(pe_ref[...].astype(BF16), w_ref[...])
        o_ref[a] = jnp.broadcast_to(cw[0:1, :DH_B] + cw[1:2, DH_B:], (8, DH_B))


def _cmp_const(pek, pev, wk, wv):
    return pl.pallas_call(
        _cmp_const_body,
        out_shape=jax.ShapeDtypeStruct((2, 8, DH_B), F32),
        name="cmp_const",
    )(pek, pev, wk, wv)


def _compress_weights(pe, phi):
    half = CMP_BLOCK // 2
    w = jnp.concatenate([phi[:half].reshape(half * DH_B, DH_B), phi[half:].reshape(half * DH_B, DH_B)], axis=1)
    pe2 = jnp.zeros((8, half * DH_B), F32)
    pe2 = pe2.at[0].set(pe[:half].reshape(-1)).at[1].set(pe[half:].reshape(-1))
    return w.astype(BF16), pe2


def _compress_prompt_body(rk_ref, rv_ref, wk_ref, wv_ref, c_ref, yk_ref, yv_ref):
    half = CMP_BLOCK // 2
    for a, (r_ref, w_ref, y_ref) in enumerate(((rk_ref, wk_ref, yk_ref), (rv_ref, wv_ref, yv_ref))):
        parts = []
        for g in range(G_B):
            parts.append(jnp.concatenate(
                [r_ref[:, (l * G_B + g) * DH_B:(l * G_B + g + 1) * DH_B] for l in range(half)], axis=1))
        a_mat = jnp.concatenate(parts, axis=0).astype(BF16)
        y = _dot(a_mat, w_ref[...])
        y = jnp.concatenate([y[:, :DH_B] + c_ref[a, 0:1, :], y[:, DH_B:]], axis=1)
        n_chunk = y.shape[0] // G_B
        for g in range(G_B):
            y_ref[g] = y[g * n_chunk:(g + 1) * n_chunk]


def _compress_prompt(kc, vc, wk, wv, const, batch, seq):
    n_chunk = seq // CMP_STRIDE
    width = CMP_STRIDE * G_B * DH_B
    rk = kc.reshape(batch, n_chunk, width)
    rv = vc.reshape(batch, n_chunk, width)
    r_spec = pl.BlockSpec((None, n_chunk, width), lambda b: (b, 0, 0))
    w_spec = pl.BlockSpec(wk.shape, lambda b: (0, 0))
    y_spec = pl.BlockSpec((None, G_B, n_chunk, 2 * DH_B), lambda b: (b, 0, 0, 0))
    y_shape = jax.ShapeDtypeStruct((batch, G_B, n_chunk, 2 * DH_B), F32)
    return pl.pallas_call(
        _compress_prompt_body,
        grid=(batch,),
        in_specs=[r_spec, r_spec, w_spec, w_spec, pl.BlockSpec(const.shape, lambda b: (0, 0, 0))],
        out_specs=[y_spec, y_spec],
        out_shape=[y_shape, y_shape],
        compiler_params=_cparams(("parallel",), 40),
        name="compress_prompt",
    )(rk, rv, wk, wv, const)


def _nsa_prompt_body(q_ref, yk_ref, yv_ref, kv_ref, gate_ref, selmap_ref, o_ref, *, tq, n_sel, n_top):
    qi = pl.program_id(1)
    rows = HPG_B * tq
    gq = G_B * DH_B
    q0 = qi * tq
    qpos_col = q0 + lax.broadcasted_iota(jnp.int32, (tq, 1), 0)

    def stack_rows(x):
        return jnp.concatenate([x] * HPG_B, axis=0)

    gates = gate_ref[...]
    for g in range(G_B):
        qg = jnp.concatenate([q_ref[:, (g * HPG_B + h) * DH_B:(g * HPG_B + h + 1) * DH_B]
                              for h in range(HPG_B)], axis=0)

        kcmp = _compressed_kv(yk_ref[g])
        vcmp = _compressed_kv(yv_ref[g])
        n_cmp = kcmp.shape[0]
        s = _dot_nt(qg, kcmp)
        cmp_end = lax.broadcasted_iota(jnp.int32, (tq, n_cmp), 1) * CMP_STRIDE + (CMP_BLOCK - 1)
        p = _masked_softmax(s, stack_rows(jnp.where(cmp_end <= qpos_col, 1.0, 0.0)) > 0.5)
        o_c = _dot(p.astype(BF16), vcmp)
        p_sum = p[0:tq]
        for h in range(1, HPG_B):
            p_sum = p_sum + p[h * tq:(h + 1) * tq]
        imp = jnp.dot(p_sum, selmap_ref[...], precision=lax.Precision.HIGHEST, preferred_element_type=F32)
        sel = _select_blocks(imp, qpos_col, n_sel, n_top).astype(BF16)

        blk_of_key = lax.shift_right_logical(lax.broadcasted_iota(jnp.int32, (n_sel, tq), 1),
                                             int(math.log2(SEL_BLOCK)))
        blk_row = lax.broadcasted_iota(jnp.int32, (n_sel, tq), 0)
        key_col = lax.broadcasted_iota(jnp.int32, (tq, tq), 1)

        def sel_step(j, carry):
            start = pl.multiple_of(j * tq, tq)
            k = kv_ref[pl.ds(start, tq), g * DH_B:(g + 1) * DH_B]
            v = kv_ref[pl.ds(start, tq), gq + g * DH_B:gq + (g + 1) * DH_B]
            expand = jnp.where(blk_row == blk_of_key + j * (tq // SEL_BLOCK), 1.0, 0.0).astype(BF16)
            ok = (_dot(sel, expand) > 0.5) & (key_col + j * tq <= qpos_col)
            bias = stack_rows(jnp.where(ok, 0.0, NEG))
            return _online_update(_dot_nt(qg, k) + bias, v, *carry)

        _, l, acc = lax.fori_loop(0, qi + 1, sel_step, _softmax_init(rows, DH_B))
        o_s = acc / l

        def win_step(dj, carry):
            j = qi - dj
            start = pl.multiple_of(j * tq, tq)
            k = kv_ref[pl.ds(start, tq), 2 * gq + g * DH_B:2 * gq + (g + 1) * DH_B]
            v = kv_ref[pl.ds(start, tq), 3 * gq + g * DH_B:3 * gq + (g + 1) * DH_B]
            dist = qpos_col - (key_col + j * tq)
            bias = stack_rows(jnp.where((dist >= 0) & (dist <= WINDOW), 0.0, NEG))
            return _online_update(_dot_nt(qg, k) + bias, v, *carry)

        n_back = WINDOW // tq
        _, l, acc = lax.fori_loop(0, jnp.minimum(qi, n_back) + 1, win_step, _softmax_init(rows, DH_B))
        o_w = acc / l

        for h in range(HPG_B):
            head = g * HPG_B + h
            rs = slice(h * tq, (h + 1) * tq)
            o = (gates[:, 3 * head:3 * head + 1] * o_c[rs] + gates[:, 3 * head + 1:3 * head + 2] * o_s[rs]
                 + gates[:, 3 * head + 2:3 * head + 3] * o_w[rs])
            o_ref[:, head * DH_B:(head + 1) * DH_B] = o.astype(o_ref.dtype)


def _nsa_prompt(q, yk, yv, nkv, gates, selmap, batch, seq, tq=256):
    n, w = q.shape
    nq = seq // tq
    n_sel = -(-seq // SEL_BLOCK)
    n_chunk = yk.shape[2]
    y_spec = pl.BlockSpec((None, G_B, n_chunk, 2 * DH_B), lambda b, i: (b, 0, 0, 0))
    return pl.pallas_call(
        functools.partial(_nsa_prompt_body, tq=tq, n_sel=n_sel, n_top=min(SEL_TOP_N, n_sel)),
        grid=(batch, nq),
        in_specs=[pl.BlockSpec((tq, w), lambda b, i: (b * nq + i, 0)),
                  y_spec, y_spec,
                  pl.BlockSpec((seq, nkv.shape[1]), lambda b, i: (b, 0)),
                  pl.BlockSpec((tq, LANES), lambda b, i: (b * nq + i, 0)),
                  pl.BlockSpec(selmap.shape, lambda b, i: (0, 0))],
        out_specs=pl.BlockSpec((tq, w), lambda b, i: (b * nq + i, 0)),
        out_shape=jax.ShapeDtypeStruct((n, w), BF16),
        compiler_params=_cparams(("parallel", "arbitrary"), 48),
        name="nsa_prompt",
    )(q, yk, yv, nkv, gates, selmap)


def _sel_map(n_chunk, n_cmp, n_sel, n_sel_pad):
    i = np.arange(n_chunk)[:, None] * CMP_STRIDE
    j = np.arange(n_sel_pad)[None, :] * SEL_BLOCK
    m = (i < j + SEL_BLOCK) & (i + CMP_BLOCK > j)
    m &= (np.arange(n_chunk)[:, None] < n_cmp) & (np.arange(n_sel_pad)[None, :] < n_sel)
    return jnp.asarray(m.astype(np.float32))


def _outproj_body(oa_ref, ob_ref, wa_ref, wb_ref, x_ref, gt_ref, g_ref, o_ref):
    m = _dot(oa_ref[...], wa_ref[...]) + _dot(ob_ref[...], wb_ref[...])
    o_ref[...] = x_ref[...] + gt_ref[...] * _rms(m, g_ref[...])


def _outproj(o_a, o_b, w_a, w_b, x, mod, k_gate, rows_per_group, g, tm):
    n, d = x.shape
    row = lambda i: (i, 0)
    const = lambda i: (0, 0)
    return pl.pallas_call(
        _outproj_body,
        grid=(n // tm,),
        in_specs=[pl.BlockSpec((tm, o_a.shape[1]), row), pl.BlockSpec((tm, o_b.shape[1]), row),
                  pl.BlockSpec(w_a.shape, const), pl.BlockSpec(w_b.shape, const),
                  pl.BlockSpec((tm, d), row),
                  _mod_spec(mod, k_gate, tm, rows_per_group),
                  pl.BlockSpec((1, d), const)],
        out_specs=pl.BlockSpec((tm, d), row),
        out_shape=jax.ShapeDtypeStruct((n, d), F32),
        compiler_params=_cparams(("parallel",), 48),
        name="outproj",
    )(o_a, o_b, w_a, w_b, x, mod, g)


def _page_specs(rows, n_pages):
    return [pl.BlockSpec((None, rows, LANES), lambda b, j, pt, r=r: (pt[b, j * n_pages + r], 0, 0))
            for r in range(n_pages)]


def _diff_decode_body(pt_ref, lamp_ref, g_ref, q_ref, knew_ref, vnew_ref, *rest, n_pages, tq, lam_init):
    k_pages = rest[:n_pages]
    v_pages = rest[n_pages:2 * n_pages]
    o_ref, q_scr, m_scr, l_scr, acc_scr = rest[2 * n_pages:]
    j = pl.program_id(1)
    rows = H_A * 2 * tq

    @pl.when(j == 0)
    def _():
        q = q_ref[...]
        lane = lax.broadcasted_iota(jnp.int32, (1, 2 * DH_A), 1)
        parts = []
        for h in range(H_A):
            qh = q[:, h * 2 * DH_A:(h + 1) * 2 * DH_A]
            parts += [jnp.where(lane < DH_A, qh, 0.0), jnp.where(lane >= DH_A, qh, 0.0)]
        q_scr[...] = jnp.concatenate(parts, axis=0).astype(BF16)
        m_scr[...] = jnp.full(m_scr.shape, NEG, F32)
        l_scr[...] = jnp.zeros(l_scr.shape, F32)
        acc_scr[...] = jnp.zeros(acc_scr.shape, F32)

    qq = q_scr[...]
    n_keys = k_pages[0].shape[0]
    head_of_row = lax.shift_right_logical(lax.broadcasted_iota(jnp.int32, (rows, n_keys), 0),
                                          int(math.log2(2 * tq)))
    head_of_key = lax.broadcasted_iota(jnp.int32, (rows, n_keys), 1) & (H_A - 1)
    same_head = head_of_row == head_of_key
    carry = (m_scr[...], l_scr[...], acc_scr[...])
    for r in range(n_pages):
        s = jnp.where(same_head, _dot_nt(qq, k_pages[r][...].astype(BF16)), NEG)
        carry = _online_update(s, v_pages[r][...].astype(BF16), *carry)
    m_scr[...], l_scr[...], acc_scr[...] = carry

    @pl.when(j == pl.num_programs(1) - 1)
    def _():
        n_new = knew_ref.shape[0]
        r_i = lax.broadcasted_iota(jnp.int32, (rows, n_new), 0)
        c_i = lax.broadcasted_iota(jnp.int32, (rows, n_new), 1)
        ok = ((lax.shift_right_logical(r_i, int(math.log2(2 * tq))) == (c_i & (H_A - 1)))
              & (lax.shift_right_logical(c_i, int(math.log2(H_A))) <= (r_i & (tq - 1))))
        s = jnp.where(ok, _dot_nt(qq, knew_ref[...].astype(BF16)), NEG)
        _, l, acc = _online_update(s, vnew_ref[...].astype(BF16), m_scr[...], l_scr[...], acc_scr[...])
        o = acc / l
        lam = _lambda(lamp_ref, lam_init)
        for h in range(H_A):
            base = h * 2 * tq
            d = o[base:base + tq] - lam * o[base + tq:base + 2 * tq]
            o_ref[:, h * 2 * DH_A:(h + 1) * 2 * DH_A] = _rms(d, g_ref[...]) * (1.0 - lam_init)


def _diff_decode(page_table, lam_p, subln_g, q, k_new, v_new, cache_k, cache_v, lam_init):
    bs, tq, w = q.shape
    n_pages_total = page_table.shape[1]
    pc = PAGES_PER_STEP
    rows = H_A * 2 * tq
    dv = 2 * DH_A
    page_rows = cache_k.shape[1]
    per_b = lambda b, j, pt: (b, 0, 0)
    const = lambda b, j, pt: (0, 0)
    grid_spec = pltpu.PrefetchScalarGridSpec(
        num_scalar_prefetch=1,
        grid=(bs, n_pages_total // pc),
        in_specs=[pl.BlockSpec(lam_p.shape, const), pl.BlockSpec((1, dv), const),
                  pl.BlockSpec((None, tq, w), per_b),
                  pl.BlockSpec((None,) + k_new.shape[1:], per_b),
                  pl.BlockSpec((None,) + v_new.shape[1:], per_b)]
        + _page_specs(page_rows, pc) + _page_specs(page_rows, pc),
        out_specs=pl.BlockSpec((None, tq, w), per_b),
        scratch_shapes=[pltpu.VMEM((rows, dv), BF16), pltpu.VMEM((rows, 1), F32),
                        pltpu.VMEM((rows, 1), F32), pltpu.VMEM((rows, dv), F32)])
    return pl.pallas_call(
        functools.partial(_diff_decode_body, n_pages=pc, tq=tq, lam_init=lam_init),
        grid_spec=grid_spec,
        out_shape=jax.ShapeDtypeStruct((bs, tq, w), F32),
        compiler_params=_cparams(("parallel", "arbitrary"), 48),
        name="diff_decode",
    )(page_table, lam_p, subln_g, q, k_new, v_new, *([cache_k] * pc), *([cache_v] * pc))


def _chunk_perm(page_tokens):
    n = page_tokens * G_B
    chunks = page_tokens // CMP_STRIDE
    out = np.arange(n)
    l, g, c = out // (G_B * chunks), (out // chunks) % G_B, out % chunks
    src = (c * CMP_STRIDE + l) * G_B + g
    perm = np.zeros((n, n), np.float32)
    perm[out, src] = 1.0
    return jnp.asarray(perm, BF16)


def _compress_decode_body(pt_ref, perm_ref, wk_ref, wv_ref, c_ref, *rest, n_pages):
    k_pages = rest[:n_pages]
    v_pages = rest[n_pages:2 * n_pages]
    yk_ref, yv_ref, a_scr = rest[2 * n_pages:]
    chunks = k_pages[0].shape[0] // (G_B * CMP_STRIDE)
    per_g = n_pages * chunks
    for a, (pages, w_ref, y_ref) in enumerate(((k_pages, wk_ref, yk_ref), (v_pages, wv_ref, yv_ref))):
        for p in range(n_pages):
            pp = _dot(perm_ref[...], pages[p][...].astype(BF16))
            for l in range(CMP_STRIDE):
                for g in range(G_B):
                    src = (l * G_B + g) * chunks
                    dst = g * per_g + p * chunks
                    a_scr[dst:dst + chunks, l * DH_B:(l + 1) * DH_B] = pp[src:src + chunks]
        y = _dot(a_scr[...].astype(BF16), w_ref[...])
        y = jnp.concatenate([y[:, :DH_B] + c_ref[a, 0:1, :], y[:, DH_B:]], axis=1)
        for g in range(G_B):
            y_ref[g] = y[g * per_g:(g + 1) * per_g]


def _compress_decode(page_table, perm, wk, wv, const, cache_k, cache_v):
    bs, n_pages_total = page_table.shape
    pc = PAGES_PER_STEP
    page_rows = cache_k.shape[1]
    chunks = page_rows // (G_B * CMP_STRIDE)
    n_chunk = n_pages_total * chunks
    const2 = lambda b, j, pt: (0, 0)
    y_spec = pl.BlockSpec((None, G_B, pc * chunks, 2 * DH_B), lambda b, j, pt: (b, 0, j, 0))
    y_shape = jax.ShapeDtypeStruct((bs, G_B, n_chunk, 2 * DH_B), F32)
    grid_spec = pltpu.PrefetchScalarGridSpec(
        num_scalar_prefetch=1,
        grid=(bs, n_pages_total // pc),
        in_specs=[pl.BlockSpec(perm.shape, const2), pl.BlockSpec(wk.shape, const2), pl.BlockSpec(wv.shape, const2),
                  pl.BlockSpec(const.shape, lambda b, j, pt: (0, 0, 0))]
        + _page_specs(page_rows, pc) + _page_specs(page_rows, pc),
        out_specs=[y_spec, y_spec],
        scratch_shapes=[pltpu.VMEM((G_B * pc * chunks, CMP_STRIDE * DH_B), F32)])
    return pl.pallas_call(
        functools.partial(_compress_decode_body, n_pages=pc),
        grid_spec=grid_spec,
        out_shape=[y_shape, y_shape],
        compiler_params=_cparams(("parallel", "arbitrary"), 40),
        name="compress_decode",
    )(page_table, perm, wk, wv, const, *([cache_k] * pc), *([cache_v] * pc))


def _nsa_decode_body(pt_ref, q_ref, yk_ref, yv_ref, selmap_ref, gate_ref, ksn_ref, vsn_ref, kwn_ref, vwn_ref,
                     wk_ref, wv_ref, *rest, n_pages, tq, past_len, n_sel, n_top):
    k_pages = rest[:n_pages]
    v_pages = rest[n_pages:2 * n_pages]
    o_ref, wko_ref, wvo_ref, q_scr, sel_scr, oc_scr, m_scr, l_scr, acc_scr = rest[2 * n_pages:]
    j = pl.program_id(1)
    rows = H_B * tq
    rows_g = HPG_B * tq
    log_tq = int(math.log2(tq))
    log_g = int(math.log2(G_B))

    def group_of_row(shape):
        return lax.shift_right_logical(lax.broadcasted_iota(jnp.int32, shape, 0), int(math.log2(rows_g)))

    def qpos_of_row(shape):
        return past_len + (lax.broadcasted_iota(jnp.int32, shape, 0) & (tq - 1))

    @pl.when(j == 0)
    def _():
        q = q_ref[...]
        q_scr[...] = jnp.concatenate([q[:, h * DH_B:(h + 1) * DH_B] for h in range(H_B)], axis=0).astype(BF16)
        qpos_col = past_len + lax.broadcasted_iota(jnp.int32, (tq, 1), 0)
        for g in range(G_B):
            qg = q_scr[g * rows_g:(g + 1) * rows_g, :]
            kcmp = _compressed_kv(yk_ref[g])
            vcmp = _compressed_kv(yv_ref[g])
            n_cmp = kcmp.shape[0]
            s = _dot_nt(qg, kcmp)
            cmp_end = lax.broadcasted_iota(jnp.int32, (rows_g, n_cmp), 1) * CMP_STRIDE + (CMP_BLOCK - 1)
            p = _masked_softmax(s, cmp_end <= qpos_of_row((rows_g, n_cmp)))
            oc_scr[g * rows_g:(g + 1) * rows_g, :] = _dot(p.astype(BF16), vcmp)
            p_sum = p[0:tq]
            for h in range(1, HPG_B):
                p_sum = p_sum + p[h * tq:(h + 1) * tq]
            imp = jnp.dot(p_sum, selmap_ref[...], precision=lax.Precision.HIGHEST, preferred_element_type=F32)
            sel = _select_blocks(imp, qpos_col, n_sel, n_top)
            for h in range(HPG_B):
                sel_scr[g * rows_g + h * tq:g * rows_g + (h + 1) * tq, :] = sel
        m_scr[...] = jnp.full(m_scr.shape, NEG, F32)
        l_scr[...] = jnp.zeros(l_scr.shape, F32)
        acc_scr[...] = jnp.zeros(acc_scr.shape, F32)

    qq = q_scr[...]
    sel_rows = sel_scr[...]
    sel_lane = lax.broadcasted_iota(jnp.int32, sel_rows.shape, 1)

    def block_selected(blk):
        return jnp.sum(jnp.where(sel_lane == blk, sel_rows, 0.0), axis=1, keepdims=True) > 0.5

    n_keys = k_pages[0].shape[0]
    page_tokens = n_keys // G_B
    blocks_per_page = page_tokens // SEL_BLOCK
    key_col = lax.broadcasted_iota(jnp.int32, (rows, n_keys), 1)
    same_group = group_of_row((rows, n_keys)) == (key_col & (G_B - 1))
    blk_in_page = lax.shift_right_logical(key_col, log_g + int(math.log2(SEL_BLOCK)))
    carry = (m_scr[...], l_scr[...], acc_scr[...])
    for r in range(n_pages):
        first_blk = (j * n_pages + r) * blocks_per_page
        chosen = (blk_in_page == 0) & block_selected(first_blk)
        for bb in range(1, blocks_per_page):
            chosen = chosen | ((blk_in_page == bb) & block_selected(first_blk + bb))
        s = jnp.where(same_group & chosen, _dot_nt(qq, k_pages[r][...].astype(BF16)), NEG)
        carry = _online_update(s, v_pages[r][...].astype(BF16), *carry)
    m_scr[...], l_scr[...], acc_scr[...] = carry

    @pl.when(j == pl.num_programs(1) - 1)
    def _():
        n_new = ksn_ref.shape[0]
        c_new = lax.broadcasted_iota(jnp.int32, (rows, n_new), 1)
        t_new = lax.shift_right_logical(c_new, log_g)
        grp_ok = group_of_row((rows, n_new)) == (c_new & (G_B - 1))
        causal = (past_len + t_new) <= qpos_of_row((rows, n_new))
        ok = grp_ok & causal & block_selected(past_len // SEL_BLOCK)
        s = jnp.where(ok, _dot_nt(qq, ksn_ref[...].astype(BF16)), NEG)
        _, l, acc = _online_update(s, vsn_ref[...].astype(BF16), m_scr[...], l_scr[...], acc_scr[...])
        o_s = acc / l

        n_win = wk_ref.shape[0]
        win_tokens = n_win // G_B
        c_w = lax.broadcasted_iota(jnp.int32, (rows, n_win), 1)
        kpos = past_len - win_tokens + lax.shift_right_logical(c_w, log_g)
        dist = qpos_of_row((rows, n_win)) - kpos
        ok_w = (group_of_row((rows, n_win)) == (c_w & (G_B - 1))) & (dist >= 0) & (dist <= WINDOW) & (kpos >= 0)
        s_w = jnp.where(ok_w, _dot_nt(qq, wk_ref[...].astype(BF16)), NEG)
        dist_n = qpos_of_row((rows, n_new)) - (past_len + t_new)
        ok_n = grp_ok & (dist_n >= 0) & (dist_n <= WINDOW)
        s_n = jnp.where(ok_n, _dot_nt(qq, kwn_ref[...].astype(BF16)), NEG)
        m = jnp.maximum(jnp.max(s_w, axis=-1, keepdims=True), jnp.max(s_n, axis=-1, keepdims=True))
        e_w = jnp.where(ok_w, jnp.exp(s_w - m), 0.0)
        e_n = jnp.where(ok_n, jnp.exp(s_n - m), 0.0)
        l_w = jnp.sum(e_w, axis=-1, keepdims=True) + jnp.sum(e_n, axis=-1, keepdims=True)
        o_w = (_dot(e_w.astype(BF16), wv_ref[...].astype(BF16))
               + _dot(e_n.astype(BF16), vwn_ref[...].astype(BF16))) / jnp.where(l_w > 0.0, l_w, 1.0)

        gates = gate_ref[...]
        o_c = oc_scr[...]
        for head in range(H_B):
            rs = slice(head * tq, (head + 1) * tq)
            o_ref[:, head * DH_B:(head + 1) * DH_B] = (
                gates[:, 3 * head:3 * head + 1] * o_c[rs] + gates[:, 3 * head + 1:3 * head + 2] * o_s[rs]
                + gates[:, 3 * head + 2:3 * head + 3] * o_w[rs])

        keep = n_win - n_new
        wko_ref[0:keep, :] = wk_ref[n_new:n_win, :]
        wko_ref[keep:n_win, :] = kwn_ref[...]
        wvo_ref[0:keep, :] = wv_ref[n_new:n_win, :]
        wvo_ref[keep:n_win, :] = vwn_ref[...]


def _nsa_decode(page_table, q, yk, yv, selmap, gates, ks_new, vs_new, kw_new, vw_new, win_k, win_v,
                cache_k, cache_v, past_len, n_sel):
    bs, tq, w = q.shape
    n_pages_total = page_table.shape[1]
    pc = PAGES_PER_STEP
    rows = H_B * tq
    page_rows = cache_k.shape[1]
    per_b3 = lambda b, j, pt: (b, 0, 0)
    per_b4 = lambda b, j, pt: (b, 0, 0, 0)
    blk3 = lambda a: pl.BlockSpec((None,) + a.shape[1:], per_b3)
    y_spec = pl.BlockSpec((None,) + yk.shape[1:], per_b4)
    grid_spec = pltpu.PrefetchScalarGridSpec(
        num_scalar_prefetch=1,
        grid=(bs, n_pages_total // pc),
        in_specs=[blk3(q), y_spec, y_spec, pl.BlockSpec(selmap.shape, lambda b, j, pt: (0, 0)), blk3(gates),
                  blk3(ks_new), blk3(vs_new), blk3(kw_new), blk3(vw_new), blk3(win_k), blk3(win_v)]
        + _page_specs(page_rows, pc) + _page_specs(page_rows, pc),
        out_specs=[blk3(q), blk3(win_k), blk3(win_v)],
        scratch_shapes=[pltpu.VMEM((rows, DH_B), BF16), pltpu.VMEM((rows, selmap.shape[1]), F32),
                        pltpu.VMEM((rows, DH_B), F32), pltpu.VMEM((rows, 1), F32), pltpu.VMEM((rows, 1), F32),
                        pltpu.VMEM((rows, DH_B), F32)])
    return pl.pallas_call(
        functools.partial(_nsa_decode_body, n_pages=pc, tq=tq, past_len=past_len, n_sel=n_sel,
                          n_top=min(SEL_TOP_N, n_sel)),
        grid_spec=grid_spec,
        out_shape=[jax.ShapeDtypeStruct((bs, tq, w), F32), jax.ShapeDtypeStruct(win_k.shape, F32),
                   jax.ShapeDtypeStruct(win_v.shape, F32)],
        compiler_params=_cparams(("parallel", "arbitrary"), 48),
        name="nsa_decode",
    )(page_table, q, yk, yv, selmap, gates, ks_new, vs_new, kw_new, vw_new, win_k, win_v,
      *([cache_k] * pc), *([cache_v] * pc))


def kernel(x_prompt, x_sample, cache_diff_k, cache_diff_v, cache_cmp_k, cache_cmp_v, cache_sel_k, cache_sel_v,
           cache_win_k, cache_win_v, page_table, c_prompt, c_sample, w_mod, b_mod, norm_g, w_in, w_out, lam_p,
           subln_g, cmp_pe, cmp_phi, ffn_gate, ffn_up, ffn_down):
    depth = w_mod.shape[0]
    assert depth == 1, "single-layer step"
    B, T, D = x_prompt.shape
    Bs, Ts, _ = x_sample.shape
    n_pool, page = cache_diff_k.shape[1], cache_diff_k.shape[2]
    past_len = page_table.shape[1] * page
    win_len = cache_win_k.shape[2]
    assert T % 512 == 0 and win_len <= T and win_len == WINDOW and past_len >= win_len
    assert past_len % SEL_BLOCK == 0 and Ts <= SEL_BLOCK and (Ts & (Ts - 1)) == 0
    assert (past_len + Ts - CMP_BLOCK) // CMP_STRIDE + 1 <= past_len // CMP_STRIDE
    lam_init = 0.8 - 0.6 * math.exp(-0.3 * 0)
    Np, Ns = B * T, Bs * Ts
    gq = G_B * DH_B

    wg, wu, wd = ffn_gate[0].astype(BF16), ffn_up[0].astype(BF16), ffn_down[0].astype(BF16)
    w_pad = jnp.pad(w_in[0].astype(BF16), ((0, 0), (0, N_IN_BLOCKS * IN_BLOCK - w_in.shape[2])))
    w_out_a, w_out_b = w_out[0, :H_A * 2 * DH_A].astype(BF16), w_out[0, H_A * 2 * DH_A:].astype(BF16)
    ng = norm_g[0].reshape(6, 1, D)
    wck, pek = _compress_weights(cmp_pe[0, 0], cmp_phi[0, 0])
    wcv, pev = _compress_weights(cmp_pe[0, 1], cmp_phi[0, 1])
    cmp_const = _cmp_const(pek, pev, wck, wcv)

    n_c = B + Bs
    c_all = jnp.pad(jnp.concatenate([c_prompt, c_sample], axis=0), ((0, -n_c % 8), (0, 0)))
    mod = _adaln_mod(c_all, w_mod[0], b_mod[0][None]).reshape(c_all.shape[0], 9, D)
    mod_p = mod[:B].transpose(1, 0, 2).reshape(9, B, 1, D)
    mod_s = jnp.repeat(mod[B:n_c].transpose(1, 0, 2), Ts, axis=1)

    def split_states(pr, batch, seq):
        return (pr["k_a"].reshape(1, batch, seq, H_A, 2 * DH_A), pr["v_a"].reshape(1, batch, seq, H_A, 2 * DH_A),
                pr["kc"].reshape(1, batch, seq, G_B, DH_B), pr["vc"].reshape(1, batch, seq, G_B, DH_B),
                pr["ks"].reshape(1, batch, seq, G_B, DH_B), pr["vs"].reshape(1, batch, seq, G_B, DH_B))

    names = ("q_a", "k_a", "k_a_bf", "v_a", "v_a_bf", "q_b", "kc", "vc", "ks", "vs", "kw", "vw", "nkv", "gates")

    tm_p = 512
    xp = x_prompt.reshape(Np, D)
    x1 = _ffn(xp, mod_p, (0, 1, 2), T, ng[0], ng[1], wg[0], wu[0], wd[0], tm_p, 512)
    pr = dict(zip(names, _inproj(x1, mod_p, 3, 4, T, ng[2], w_pad, _rope_tables(np.arange(T)), tm_p)))
    o_a = _diff_prompt(pr["q_a"], pr["k_a_bf"], pr["v_a_bf"], lam_p[0], subln_g[0][None], B, T, lam_init)
    yk, yv = _compress_prompt(pr["kc"], pr["vc"], wck, wcv, cmp_const, B, T)
    n_cmp_p = (T - CMP_BLOCK) // CMP_STRIDE + 1
    n_sel_p = -(-T // SEL_BLOCK)
    selmap_p = _sel_map(T // CMP_STRIDE, n_cmp_p, n_sel_p, n_sel_p)
    o_b = _nsa_prompt(pr["q_b"], yk, yv, pr["nkv"], pr["gates"], selmap_p, B, T)
    x2 = _outproj(o_a, o_b, w_out_a, w_out_b, x1, mod_p, 5, T, ng[3], tm_p)
    y_p = _ffn(x2, mod_p, (6, 7, 8), T, ng[4], ng[5], wg[1], wu[1], wd[1], tm_p, 512).reshape(B, T, D)
    st_p = split_states(pr, B, T)
    wk_p = pr["kw"].reshape(1, B, T, G_B, DH_B)[:, :, T - win_len:]
    wv_p = pr["vw"].reshape(1, B, T, G_B, DH_B)[:, :, T - win_len:]

    xs = x_sample.reshape(Ns, D)
    x1s = _ffn(xs, mod_s, (0, 1, 2), 1, ng[0], ng[1], wg[0], wu[0], wd[0], Ns, 512)
    pos_s = np.tile(past_len + np.arange(Ts), Bs)
    ps = dict(zip(names, _inproj(x1s, mod_s, 3, 4, 1, ng[2], w_pad, _rope_tables(pos_s), Ns)))
    ck = cache_diff_k[0].reshape(n_pool, page * H_A, 2 * DH_A)
    cv = cache_diff_v[0].reshape(n_pool, page * H_A, 2 * DH_A)
    o_as = _diff_decode(page_table, lam_p[0], subln_g[0][None],
                        ps["q_a"].astype(F32).reshape(Bs, Ts, H_A * 2 * DH_A),
                        ps["k_a"].reshape(Bs, Ts * H_A, 2 * DH_A), ps["v_a"].reshape(Bs, Ts * H_A, 2 * DH_A),
                        ck, cv, lam_init)
    pool3 = lambda c: c[0].reshape(n_pool, page * G_B, DH_B)
    yks, yvs = _compress_decode(page_table, _chunk_perm(page), wck, wcv, cmp_const,
                                pool3(cache_cmp_k), pool3(cache_cmp_v))
    tk_s = past_len + Ts
    n_cmp_s = (tk_s - CMP_BLOCK) // CMP_STRIDE + 1
    n_sel_s = -(-tk_s // SEL_BLOCK)
    selmap_s = _sel_map(past_len // CMP_STRIDE, n_cmp_s, n_sel_s, -(-n_sel_s // LANES) * LANES)
    new3 = lambda a: a.reshape(Bs, Ts * G_B, DH_B)
    o_bs, wk_s, wv_s = _nsa_decode(
        page_table, ps["q_b"].astype(F32).reshape(Bs, Ts, H_B * DH_B), yks, yvs, selmap_s,
        ps["gates"].reshape(Bs, Ts, LANES), new3(ps["ks"]), new3(ps["vs"]), new3(ps["kw"]), new3(ps["vw"]),
        cache_win_k[0].reshape(Bs, win_len * G_B, DH_B), cache_win_v[0].reshape(Bs, win_len * G_B, DH_B),
        pool3(cache_sel_k), pool3(cache_sel_v), past_len, n_sel_s)
    x2s = _outproj(o_as.reshape(Ns, -1).astype(BF16), o_bs.reshape(Ns, -1).astype(BF16), w_out_a, w_out_b,
                   x1s, mod_s, 5, 1, ng[3], Ns)
    y_s = _ffn(x2s, mod_s, (6, 7, 8), 1, ng[4], ng[5], wg[1], wu[1], wd[1], Ns, 512).reshape(Bs, Ts, D)
    st_s = split_states(ps, Bs, Ts)
    wk_s = wk_s.reshape(1, Bs, win_len, G_B, DH_B)
    wv_s = wv_s.reshape(1, Bs, win_len, G_B, DH_B)

    return (y_p, y_s) + st_p + (wk_p, wv_p) + st_s + (wk_s, wv_s)
```

```python
import functools
import math

import numpy as np
import jax
import jax.numpy as jnp
from jax import lax
from jax.experimental import pallas as pl
from jax.experimental.pallas import tpu as pltpu

F32 = jnp.float32
BF16 = jnp.bfloat16

DH_A = 64
H_A = 8
DH_B = 128
H_B = 8
G_B = 2
HPG_B = H_B // G_B
CMP_BLOCK = 32
CMP_STRIDE = 16
SEL_BLOCK = 64
SEL_TOP_N = 16
N_LOCAL = 2
WINDOW = 512
ROPE_THETA = 10000.0
EPS = 1e-6
NEG = -1e30
FORCE = 1e4
LOG2E = math.log2(math.e)

LANES = 128
MIB = 1024 * 1024
PAGES_PER_STEP = 8
GROUP_PAGES_PER_STEP = 16
FFN_TF = 1024


def _dot(a, b):
    return jnp.dot(a, b, preferred_element_type=F32)


def _dot_nt(a, b):
    return lax.dot_general(a, b, (((1,), (1,)), ((), ())), preferred_element_type=F32)


def _rms(x, g):
    return x * lax.rsqrt(jnp.mean(x * x, axis=-1, keepdims=True) + EPS) * g


def _cparams(sem, vmem_mib):
    return pltpu.CompilerParams(dimension_semantics=sem, vmem_limit_bytes=vmem_mib * MIB)


def _mod_body(c_ref, w_ref, b_ref, o_ref):
    c = c_ref[...]
    a = (c * jax.nn.sigmoid(c)).astype(BF16)
    o_ref[...] = _dot(a, w_ref[...].astype(BF16)) + b_ref[...]


def _adaln_mod(c, w_mod, b_mod):
    R, D = c.shape
    n_out = w_mod.shape[1]
    tn = 1024
    return pl.pallas_call(
        _mod_body,
        grid=(n_out // tn,),
        in_specs=[pl.BlockSpec((R, D), lambda j: (0, 0)),
                  pl.BlockSpec((D, tn), lambda j: (0, j)),
                  pl.BlockSpec((1, tn), lambda j: (0, j))],
        out_specs=pl.BlockSpec((R, tn), lambda j: (0, j)),
        out_shape=jax.ShapeDtypeStruct((R, n_out), F32),
        compiler_params=_cparams(("arbitrary",), 40),
        name="adaln_mod",
    )(c, w_mod, b_mod)


def _mod_spec(mod, k, tm, rows_per_group):
    if mod.ndim == 4:
        return pl.BlockSpec((None, None, 1, mod.shape[-1]),
                            lambda i, *_, k=k: (k, (i * tm) // rows_per_group, 0, 0))
    return pl.BlockSpec((None, tm, mod.shape[-1]), lambda i, *_, k=k: (k, i, 0))


def _ffn_body(x_ref, sh_ref, sc_ref, gt_ref, gpre_ref, gpost_ref, wg_ref, wu_ref, wd_ref, o_ref,
              h_scr, acc_scr, *, n_chunk):
    f = pl.program_id(1)

    @pl.when(f == 0)
    def _():
        h = _rms(x_ref[...], gpre_ref[...]) * (1.0 + sc_ref[...]) + sh_ref[...]
        h_scr[...] = h.astype(BF16)

    h = h_scr[...]
    g = _dot(h, wg_ref[...])
    u = _dot(h, wu_ref[...])
    a = (g * jax.nn.sigmoid(g) * u).astype(BF16)
    d = acc_scr.shape[1]
    cw = d // n_chunk

    @pl.when(f == 0)
    def _():
        for n in range(n_chunk):
            acc_scr[:, n * cw:(n + 1) * cw] = _dot(a, wd_ref[:, n * cw:(n + 1) * cw])

    @pl.when(f > 0)
    def _():
        for n in range(n_chunk):
            acc_scr[:, n * cw:(n + 1) * cw] += _dot(a, wd_ref[:, n * cw:(n + 1) * cw])

    @pl.when(f == pl.num_programs(1) - 1)
    def _():
        o_ref[...] = x_ref[...] + 0.5 * gt_ref[...] * _rms(acc_scr[...], gpost_ref[...])


def _ffn(x, mod, ks, rows_per_group, g_pre, g_post, wg, wu, wd, tm, tf):
    n, d = x.shape
    dff = wg.shape[1]
    row = lambda i, f: (i, 0)
    const = lambda i, f: (0, 0)
    return pl.pallas_call(
        functools.partial(_ffn_body, n_chunk=4),
        grid=(n // tm, dff // tf),
        in_specs=[pl.BlockSpec((tm, d), row),
                  _mod_spec(mod, ks[0], tm, rows_per_group),
                  _mod_spec(mod, ks[1], tm, rows_per_group),
                  _mod_spec(mod, ks[2], tm, rows_per_group),
                  pl.BlockSpec((1, d), const), pl.BlockSpec((1, d), const),
                  pl.BlockSpec((d, tf), lambda i, f: (0, f)),
                  pl.BlockSpec((d, tf), lambda i, f: (0, f)),
                  pl.BlockSpec((tf, d), lambda i, f: (f, 0))],
        out_specs=pl.BlockSpec((tm, d), row),
        out_shape=jax.ShapeDtypeStruct((n, d), F32),
        scratch_shapes=[pltpu.VMEM((tm, d), BF16), pltpu.VMEM((tm, d), F32)],
        compiler_params=_cparams(("parallel", "arbitrary"), 60),
        name="ffn",
    )(x, mod, mod, mod, g_pre, g_post, wg, wu, wd)


IN_BLOCK = 1024
N_IN_BLOCKS = 6


def _inproj_body(x_ref, sh_ref, sc_ref, g_ref, w_ref, ca_ref, sa_ref, cb_ref, sb_ref,
                 qa_ref, ka_ref, kab_ref, va_ref, vab_ref, qb_ref, kc_ref, vc_ref, ks_ref, vs_ref,
                 kw_ref, vw_ref, nkv_ref, gate_ref, h_scr):
    j = pl.program_id(1)

    @pl.when(j == 0)
    def _():
        h = _rms(x_ref[...], g_ref[...]) * (1.0 + sc_ref[...]) + sh_ref[...]
        h_scr[...] = h.astype(BF16)

    lane = lax.broadcasted_iota(jnp.int32, (1, LANES), 1)
    low32 = (lane & (DH_A - 1)) < (DH_A // 2)

    def cols(c):
        y = _dot(h_scr[...], w_ref[:, c * 256:(c + 1) * 256])
        return y[:, :LANES], y[:, LANES:]

    def rope64(y):
        partner = jnp.where(low32, pltpu.roll(y, LANES - DH_A // 2, 1), pltpu.roll(y, DH_A // 2, 1))
        return y * ca_ref[...] + partner * sa_ref[...]

    def rope128(y):
        return y * cb_ref[...] + pltpu.roll(y, DH_B // 2, 1) * sb_ref[...]

    def tile(t):
        return slice(t * LANES, (t + 1) * LANES)

    @pl.when(j == 0)
    def _():
        for c in range(4):
            for t, y in enumerate(cols(c)):
                qa_ref[:, tile(2 * c + t)] = (rope64(y) * (DH_A ** -0.5 * LOG2E)).astype(BF16)

    @pl.when(j == 1)
    def _():
        for c in range(4):
            for t, y in enumerate(cols(c)):
                r = rope64(y)
                ka_ref[:, tile(2 * c + t)] = r
                kab_ref[:, tile(2 * c + t)] = r.astype(BF16)

    @pl.when(j == 2)
    def _():
        for c in range(4):
            for t, y in enumerate(cols(c)):
                va_ref[:, tile(2 * c + t)] = y
                vab_ref[:, tile(2 * c + t)] = y.astype(BF16)

    @pl.when(j == 3)
    def _():
        for c in range(4):
            for t, y in enumerate(cols(c)):
                qb_ref[:, tile(2 * c + t)] = (rope128(y) * (DH_B ** -0.5 * LOG2E)).astype(BF16)

    @pl.when(j == 4)
    def _():
        for c, (ref, roped) in enumerate(((kc_ref, True), (vc_ref, False), (ks_ref, True), (vs_ref, False))):
            for t, y in enumerate(cols(c)):
                r = rope128(y) if roped else y
                ref[:, tile(t)] = r
                if c >= 2:
                    nkv_ref[:, tile(2 * (c - 2) + t)] = r.astype(BF16)

    @pl.when(j == 5)
    def _():
        for c, (ref, roped) in enumerate(((kw_ref, True), (vw_ref, False))):
            for t, y in enumerate(cols(c)):
                r = rope128(y) if roped else y
                ref[:, tile(t)] = r
                nkv_ref[:, tile(4 + 2 * c + t)] = r.astype(BF16)
        gate_ref[...] = jax.nn.sigmoid(cols(2)[0])


def _inproj(x, mod, k_shift, k_scale, rows_per_group, g, w_pad, tabs, tm):
    n, d = x.shape
    gq = G_B * DH_B
    n_tab = tabs[0].shape[0] // tm
    row = lambda i, j: (i, 0)
    const = lambda i, j: (0, 0)
    tab = pl.BlockSpec((tm, LANES), lambda i, j: (i % n_tab, 0))
    wide = lambda dt: jax.ShapeDtypeStruct((n, IN_BLOCK), dt)
    narrow = jax.ShapeDtypeStruct((n, gq), F32)
    out_shape = [wide(BF16), wide(F32), wide(BF16), wide(F32), wide(BF16), wide(BF16)] + [narrow] * 6 + [
        wide(BF16), jax.ShapeDtypeStruct((n, LANES), F32)]
    out_specs = [pl.BlockSpec((tm, s.shape[1]), row) for s in out_shape]
    return pl.pallas_call(
        _inproj_body,
        grid=(n // tm, N_IN_BLOCKS),
        in_specs=[pl.BlockSpec((tm, d), row),
                  _mod_spec(mod, k_shift, tm, rows_per_group),
                  _mod_spec(mod, k_scale, tm, rows_per_group),
                  pl.BlockSpec((1, d), const),
                  pl.BlockSpec((d, IN_BLOCK), lambda i, j: (0, j)),
                  tab, tab, tab, tab],
        out_specs=out_specs,
        out_shape=out_shape,
        scratch_shapes=[pltpu.VMEM((tm, d), BF16)],
        compiler_params=_cparams(("parallel", "arbitrary"), 56),
        name="inproj",
    )(x, mod, mod, g, w_pad, *tabs)


def _rope_tables(pos):
    pos = np.asarray(pos, np.float64)[:, None]

    def tables(head_dim):
        half = head_dim // 2
        inv = ROPE_THETA ** (-np.arange(half, dtype=np.float64) / half)
        ang = pos * inv[None, :]
        cos = np.concatenate([np.cos(ang), np.cos(ang)], axis=1)
        sin = np.concatenate([-np.sin(ang), np.sin(ang)], axis=1)
        rep = LANES // head_dim
        return (jnp.asarray(np.tile(cos, (1, rep)), F32), jnp.asarray(np.tile(sin, (1, rep)), F32))

    ca, sa = tables(DH_A)
    cb, sb = tables(DH_B)
    return ca, sa, cb, sb


def _lambda(lamp_ref, lam_init):
    lp = lamp_ref[...]
    a = jnp.sum(lp[0:1] * lp[1:2], axis=1, keepdims=True)
    b = jnp.sum(lp[2:3] * lp[3:4], axis=1, keepdims=True)
    return jnp.exp(a) - jnp.exp(b) + lam_init


def _online_update(s, v, m, l, acc):
    m_new = jnp.maximum(m, jnp.max(s, axis=-1, keepdims=True))
    p = jnp.exp2(s - m_new)
    alpha = jnp.exp2(m - m_new)
    l = alpha * l + jnp.sum(p, axis=-1, keepdims=True)
    acc = alpha * acc + _dot(p.astype(BF16), v)
    return m_new, l, acc


def _online_update_pages(ss, vs, m, l, acc):
    m_blk = jnp.max(ss[0], axis=-1, keepdims=True)
    for s in ss[1:]:
        m_blk = jnp.maximum(m_blk, jnp.max(s, axis=-1, keepdims=True))
    m_new = jnp.maximum(m, m_blk)
    alpha = jnp.exp2(m - m_new)
    l = alpha * l
    acc = alpha * acc
    for s, v in zip(ss, vs):
        p = jnp.exp2(s - m_new)
        l = l + jnp.sum(p, axis=-1, keepdims=True)
        acc = acc + _dot(p.astype(BF16), v)
    return m_new, l, acc


def _softmax_init(rows, dv):
    return (jnp.full((rows, 1), NEG, F32), jnp.zeros((rows, 1), F32), jnp.zeros((rows, dv), F32))


def _masked_softmax(s, mask):
    sm = jnp.where(mask, s, NEG)
    m = jnp.max(sm, axis=-1, keepdims=True)
    e = jnp.where(mask, jnp.exp2(sm - m), 0.0)
    l = jnp.sum(e, axis=-1, keepdims=True)
    return e / jnp.where(l > 0.0, l, 1.0)


def _select_blocks(imp, qpos, n_blocks, n_top):
    shape = imp.shape
    blk = lax.broadcasted_iota(jnp.int32, shape, 1)
    rel = lax.shift_right_logical(qpos, int(math.log2(SEL_BLOCK))) - blk
    visible = rel >= 0
    forced = (blk == 0) | (visible & (rel < N_LOCAL))
    score = jnp.where(visible, jnp.where(forced, FORCE + imp, imp), NEG)
    rank = jnp.zeros(shape, F32)
    for i in range(n_blocks):
        col = score[:, i:i + 1]
        beats = (col > score) | ((col == score) & (blk > i))
        rank = rank + jnp.where(beats, 1.0, 0.0)
    return jnp.where((rank < n_top) & visible, 1.0, 0.0)


def _select_blocks_t(imp_t, qpos_row, n_blocks, n_top):
    shape = imp_t.shape
    blk = lax.broadcasted_iota(jnp.int32, shape, 0)
    rel = lax.shift_right_logical(qpos_row, int(math.log2(SEL_BLOCK))) - blk
    visible = rel >= 0
    forced = (blk == 0) | (visible & (rel < N_LOCAL))
    score = jnp.where(visible, jnp.where(forced, FORCE + imp_t, imp_t), NEG)
    rank = jnp.zeros(shape, F32)
    for i in range(n_blocks):
        row = score[i:i + 1, :]
        beats = (row > score) | ((row == score) & (blk > i))
        rank = rank + jnp.where(beats, 1.0, 0.0)
    return jnp.where((rank < n_top) & visible, 1.0, 0.0)


def _compressed_kv(y):
    n = y.shape[0]
    return (y[:, :DH_B] + pltpu.roll(y[:, DH_B:], n - 1, 0)).astype(BF16)


def _diff_prompt_body(lamp_ref, g_ref, q_ref, k_ref, v_ref, o_ref, *, tq, lam_init):
    qi = pl.program_id(2)
    lam = _lambda(lamp_ref, lam_init)
    q = q_ref[...]
    lane = lax.broadcasted_iota(jnp.int32, (1, 2 * DH_A), 1)
    zero = jnp.zeros_like(q)
    qq = jnp.concatenate([jnp.where(lane < DH_A, q, zero), jnp.where(lane >= DH_A, q, zero)], axis=0)

    def chunk(j):
        start = pl.multiple_of(j * tq, tq)
        return k_ref[pl.ds(start, tq), :], v_ref[pl.ds(start, tq), :]

    def body(j, carry):
        k, v = chunk(j)
        return _online_update(_dot_nt(qq, k), v, *carry)

    carry = lax.fori_loop(0, qi, body, _softmax_init(2 * tq, 2 * DH_A))
    k, v = chunk(qi)
    s = _dot_nt(qq, k)
    r = lax.broadcasted_iota(jnp.int32, s.shape, 0) & (tq - 1)
    c = lax.broadcasted_iota(jnp.int32, s.shape, 1)
    m, l, acc = _online_update(jnp.where(c <= r, s, NEG), v, *carry)
    o = acc / l
    o_ref[...] = (_rms(o[:tq] - lam * o[tq:], g_ref[...]) * (1.0 - lam_init)).astype(o_ref.dtype)


def _diff_prompt(q, k, v, lam_p, subln_g, batch, seq, lam_init, tq=512):
    n, w = q.shape
    dv = 2 * DH_A
    nq = seq // tq
    return pl.pallas_call(
        functools.partial(_diff_prompt_body, tq=tq, lam_init=lam_init),
        grid=(batch, H_A, nq),
        in_specs=[pl.BlockSpec(lam_p.shape, lambda b, h, i: (0, 0)),
                  pl.BlockSpec((1, dv), lambda b, h, i: (0, 0)),
                  pl.BlockSpec((tq, dv), lambda b, h, i: (b * nq + i, h)),
                  pl.BlockSpec((seq, dv), lambda b, h, i: (b, h)),
                  pl.BlockSpec((seq, dv), lambda b, h, i: (b, h))],
        out_specs=pl.BlockSpec((tq, dv), lambda b, h, i: (b * nq + i, h)),
        out_shape=jax.ShapeDtypeStruct((n, w), BF16),
        compiler_params=_cparams(("parallel", "parallel", "arbitrary"), 48),
        name="diff_prompt",
    )(lam_p, subln_g, q, k, v)


def _cmp_const_body(pek_ref, pev_ref, wk_ref, wv_ref, o_ref):
    for a, (pe_ref, w_ref) in enumerate(((pek_ref, wk_ref), (pev_ref, wv_ref))):
        cw = _dot(pe_ref[...].astype(BF16), w_ref[...])
        o_ref[a] = jnp.broadcast_to(cw[0:1, :DH_B] + cw[1:2, DH_B:], (8, DH_B))


def _cmp_const(pek, pev, wk, wv):
    return pl.pallas_call(
        _cmp_const_body,
        out_shape=jax.ShapeDtypeStruct((2, 8, DH_B), F32),
        name="cmp_const",
    )(pek, pev, wk, wv)


def _compress_weights(pe, phi):
    half = CMP_BLOCK // 2
    w = jnp.concatenate([phi[:half].reshape(half * DH_B, DH_B), phi[half:].reshape(half * DH_B, DH_B)], axis=1)
    pe2 = jnp.zeros((8, half * DH_B), F32)
    pe2 = pe2.at[0].set(pe[:half].reshape(-1)).at[1].set(pe[half:].reshape(-1))
    return w.astype(BF16), pe2


def _compress_prompt_body(rk_ref, rv_ref, wk_ref, wv_ref, c_ref, yk_ref, yv_ref):
    half = CMP_BLOCK // 2
    for a, (r_ref, w_ref, y_ref) in enumerate(((rk_ref, wk_ref, yk_ref), (rv_ref, wv_ref, yv_ref))):
        parts = []
        for g in range(G_B):
            parts.append(jnp.concatenate(
                [r_ref[:, (l * G_B + g) * DH_B:(l * G_B + g + 1) * DH_B] for l in range(half)], axis=1))
        a_mat = jnp.concatenate(parts, axis=0).astype(BF16)
        y = _dot(a_mat, w_ref[...])
        y = jnp.concatenate([y[:, :DH_B] + c_ref[a, 0:1, :], y[:, DH_B:]], axis=1)
        n_chunk = y.shape[0] // G_B
        for g in range(G_B):
            y_ref[g] = y[g * n_chunk:(g + 1) * n_chunk]


def _compress_prompt(kc, vc, wk, wv, const, batch, seq):
    n_chunk = seq // CMP_STRIDE
    width = CMP_STRIDE * G_B * DH_B
    rk = kc.reshape(batch, n_chunk, width)
    rv = vc.reshape(batch, n_chunk, width)
    r_spec = pl.BlockSpec((None, n_chunk, width), lambda b: (b, 0, 0))
    w_spec = pl.BlockSpec(wk.shape, lambda b: (0, 0))
    y_spec = pl.BlockSpec((None, G_B, n_chunk, 2 * DH_B), lambda b: (b, 0, 0, 0))
    y_shape = jax.ShapeDtypeStruct((batch, G_B, n_chunk, 2 * DH_B), F32)
    return pl.pallas_call(
        _compress_prompt_body,
        grid=(batch,),
        in_specs=[r_spec, r_spec, w_spec, w_spec, pl.BlockSpec(const.shape, lambda b: (0, 0, 0))],
        out_specs=[y_spec, y_spec],
        out_shape=[y_shape, y_shape],
        compiler_params=_cparams(("parallel",), 40),
        name="compress_prompt",
    )(rk, rv, wk, wv, const)


def _nsa_prompt_body(q_ref, yk_ref, yv_ref, kv_ref, gate_ref, selmap_ref, o_ref, *, tq, n_sel, n_top):
    qi = pl.program_id(1)
    rows = HPG_B * tq
    gq = G_B * DH_B
    q0 = qi * tq
    qpos_col = q0 + lax.broadcasted_iota(jnp.int32, (tq, 1), 0)
    qpos_row = q0 + lax.broadcasted_iota(jnp.int32, (1, tq), 1)

    def stack_rows(x):
        return jnp.concatenate([x] * HPG_B, axis=0)

    gates = gate_ref[...]
    for g in range(G_B):
        qg = jnp.concatenate([q_ref[:, (g * HPG_B + h) * DH_B:(g * HPG_B + h + 1) * DH_B]
                              for h in range(HPG_B)], axis=0)

        kcmp = _compressed_kv(yk_ref[g])
        vcmp = _compressed_kv(yv_ref[g])
        n_cmp = kcmp.shape[0]
        s = _dot_nt(qg, kcmp)
        cmp_end = lax.broadcasted_iota(jnp.int32, (tq, n_cmp), 1) * CMP_STRIDE + (CMP_BLOCK - 1)
        p = _masked_softmax(s, stack_rows(jnp.where(cmp_end <= qpos_col, 1.0, 0.0)) > 0.5)
        o_c = _dot(p.astype(BF16), vcmp)
        p_sum = p[0:tq]
        for h in range(1, HPG_B):
            p_sum = p_sum + p[h * tq:(h + 1) * tq]
        imp_t = lax.dot_general(selmap_ref[...], p_sum, (((1,), (1,)), ((), ())),
                                precision=lax.Precision.HIGHEST, preferred_element_type=F32)
        sel_t = _select_blocks_t(imp_t, qpos_row, n_sel, n_top)
        sel = jnp.concatenate([sel_t, jnp.zeros((LANES - n_sel, tq), F32)], axis=0).T.astype(BF16)

        blk_of_key = lax.shift_right_logical(lax.broadcasted_iota(jnp.int32, (LANES, tq), 1),
                                             int(math.log2(SEL_BLOCK)))
        blk_row = lax.broadcasted_iota(jnp.int32, (LANES, tq), 0)
        key_col = lax.broadcasted_iota(jnp.int32, (tq, tq), 1)

        def sel_step(j, carry):
            start = pl.multiple_of(j * tq, tq)
            k = kv_ref[pl.ds(start, tq), g * DH_B:(g + 1) * DH_B]
            v = kv_ref[pl.ds(start, tq), gq + g * DH_B:gq + (g + 1) * DH_B]
            expand = jnp.where(blk_row == blk_of_key + j * (tq // SEL_BLOCK), 1.0, 0.0).astype(BF16)
            ok = (_dot(sel, expand) > 0.5) & (key_col + j * tq <= qpos_col)
            bias = stack_rows(jnp.where(ok, 0.0, NEG))
            return _online_update(_dot_nt(qg, k) + bias, v, *carry)

        _, l, acc = lax.fori_loop(0, qi + 1, sel_step, _softmax_init(rows, DH_B))
        o_s = acc / l

        def win_step(dj, carry):
            j = qi - dj
            start = pl.multiple_of(j * tq, tq)
            k = kv_ref[pl.ds(start, tq), 2 * gq + g * DH_B:2 * gq + (g + 1) * DH_B]
            v = kv_ref[pl.ds(start, tq), 3 * gq + g * DH_B:3 * gq + (g + 1) * DH_B]
            dist = qpos_col - (key_col + j * tq)
            bias = stack_rows(jnp.where((dist >= 0) & (dist <= WINDOW), 0.0, NEG))
            return _online_update(_dot_nt(qg, k) + bias, v, *carry)

        n_back = WINDOW // tq
        _, l, acc = lax.fori_loop(0, jnp.minimum(qi, n_back) + 1, win_step, _softmax_init(rows, DH_B))
        o_w = acc / l

        for h in range(HPG_B):
            head = g * HPG_B + h
            rs = slice(h * tq, (h + 1) * tq)
            o = (gates[:, 3 * head:3 * head + 1] * o_c[rs] + gates[:, 3 * head + 1:3 * head + 2] * o_s[rs]
                 + gates[:, 3 * head + 2:3 * head + 3] * o_w[rs])
            o_ref[:, head * DH_B:(head + 1) * DH_B] = o.astype(o_ref.dtype)


def _nsa_prompt(q, yk, yv, nkv, gates, selmap, batch, seq, tq=256):
    n, w = q.shape
    nq = seq // tq
    n_sel = -(-seq // SEL_BLOCK)
    n_chunk = yk.shape[2]
    y_spec = pl.BlockSpec((None, G_B, n_chunk, 2 * DH_B), lambda b, i: (b, 0, 0, 0))
    return pl.pallas_call(
        functools.partial(_nsa_prompt_body, tq=tq, n_sel=n_sel, n_top=min(SEL_TOP_N, n_sel)),
        grid=(batch, nq),
        in_specs=[pl.BlockSpec((tq, w), lambda b, i: (b * nq + i, 0)),
                  y_spec, y_spec,
                  pl.BlockSpec((seq, nkv.shape[1]), lambda b, i: (b, 0)),
                  pl.BlockSpec((tq, LANES), lambda b, i: (b * nq + i, 0)),
                  pl.BlockSpec(selmap.shape, lambda b, i: (0, 0))],
        out_specs=pl.BlockSpec((tq, w), lambda b, i: (b * nq + i, 0)),
        out_shape=jax.ShapeDtypeStruct((n, w), BF16),
        compiler_params=_cparams(("parallel", "arbitrary"), 48),
        name="nsa_prompt",
    )(q, yk, yv, nkv, gates, selmap)


def _sel_map(n_chunk, n_cmp, n_sel, n_sel_pad):
    i = np.arange(n_chunk)[:, None] * CMP_STRIDE
    j = np.arange(n_sel_pad)[None, :] * SEL_BLOCK
    m = (i < j + SEL_BLOCK) & (i + CMP_BLOCK > j)
    m &= (np.arange(n_chunk)[:, None] < n_cmp) & (np.arange(n_sel_pad)[None, :] < n_sel)
    return jnp.asarray(m.astype(np.float32))


def _outproj_body(oa_ref, ob_ref, wa_ref, wb_ref, x_ref, gt_ref, g_ref, o_ref):
    m = _dot(oa_ref[...], wa_ref[...]) + _dot(ob_ref[...], wb_ref[...])
    o_ref[...] = x_ref[...] + gt_ref[...] * _rms(m, g_ref[...])


def _outproj(o_a, o_b, w_a, w_b, x, mod, k_gate, rows_per_group, g, tm):
    n, d = x.shape
    row = lambda i: (i, 0)
    const = lambda i: (0, 0)
    return pl.pallas_call(
        _outproj_body,
        grid=(n // tm,),
        in_specs=[pl.BlockSpec((tm, o_a.shape[1]), row), pl.BlockSpec((tm, o_b.shape[1]), row),
                  pl.BlockSpec(w_a.shape, const), pl.BlockSpec(w_b.shape, const),
                  pl.BlockSpec((tm, d), row),
                  _mod_spec(mod, k_gate, tm, rows_per_group),
                  pl.BlockSpec((1, d), const)],
        out_specs=pl.BlockSpec((tm, d), row),
        out_shape=jax.ShapeDtypeStruct((n, d), F32),
        compiler_params=_cparams(("parallel",), 48),
        name="outproj",
    )(o_a, o_b, w_a, w_b, x, mod, g)


def _page_specs(rows, n_pages):
    return [pl.BlockSpec((None, rows, LANES), lambda b, j, pt, r=r: (pt[b, j * n_pages + r], 0, 0))
            for r in range(n_pages)]


def _diff_decode_body(pt_ref, lamp_ref, g_ref, q_ref, knew_ref, vnew_ref, *rest, n_pages, tq, lam_init):
    k_pages = rest[:n_pages]
    v_pages = rest[n_pages:2 * n_pages]
    o_ref, q_scr, bias_scr, m_scr, l_scr, acc_scr = rest[2 * n_pages:]
    j = pl.program_id(1)
    rows = H_A * 2 * tq
    n_keys = k_pages[0].shape[0]

    @pl.when(j == 0)
    def _():
        q = q_ref[...]
        lane = lax.broadcasted_iota(jnp.int32, (1, 2 * DH_A), 1)
        parts = []
        for h in range(H_A):
            qh = q[:, h * 2 * DH_A:(h + 1) * 2 * DH_A]
            parts += [jnp.where(lane < DH_A, qh, 0.0), jnp.where(lane >= DH_A, qh, 0.0)]
        q_scr[...] = jnp.concatenate(parts, axis=0).astype(BF16)
        head_of_row = lax.shift_right_logical(lax.broadcasted_iota(jnp.int32, (rows, n_keys), 0),
                                              int(math.log2(2 * tq)))
        head_of_key = lax.broadcasted_iota(jnp.int32, (rows, n_keys), 1) & (H_A - 1)
        bias_scr[...] = jnp.where(head_of_row == head_of_key, 0.0, NEG)
        m_scr[...] = jnp.full(m_scr.shape, NEG, F32)
        l_scr[...] = jnp.zeros(l_scr.shape, F32)
        acc_scr[...] = jnp.zeros(acc_scr.shape, F32)

    qq = q_scr[...]
    ss = [_dot_nt(qq, k_pages[r][...].astype(BF16)) + bias_scr[...] for r in range(n_pages)]
    vs = [v_pages[r][...].astype(BF16) for r in range(n_pages)]
    m_scr[...], l_scr[...], acc_scr[...] = _online_update_pages(ss, vs, m_scr[...], l_scr[...], acc_scr[...])

    @pl.when(j == pl.num_programs(1) - 1)
    def _():
        n_new = knew_ref.shape[0]
        r_i = lax.broadcasted_iota(jnp.int32, (rows, n_new), 0)
        c_i = lax.broadcasted_iota(jnp.int32, (rows, n_new), 1)
        ok = ((lax.shift_right_logical(r_i, int(math.log2(2 * tq))) == (c_i & (H_A - 1)))
              & (lax.shift_right_logical(c_i, int(math.log2(H_A))) <= (r_i & (tq - 1))))
        s = jnp.where(ok, _dot_nt(qq, knew_ref[...].astype(BF16)), NEG)
        _, l, acc = _online_update(s, vnew_ref[...].astype(BF16), m_scr[...], l_scr[...], acc_scr[...])
        o = acc / l
        lam = _lambda(lamp_ref, lam_init)
        for h in range(H_A):
            base = h * 2 * tq
            d = o[base:base + tq] - lam * o[base + tq:base + 2 * tq]
            o_ref[:, h * 2 * DH_A:(h + 1) * 2 * DH_A] = _rms(d, g_ref[...]) * (1.0 - lam_init)


def _diff_decode(page_table, lam_p, subln_g, q, k_new, v_new, cache_k, cache_v, lam_init):
    bs, tq, w = q.shape
    n_pages_total = page_table.shape[1]
    pc = PAGES_PER_STEP
    rows = H_A * 2 * tq
    dv = 2 * DH_A
    page_rows = cache_k.shape[1]
    per_b = lambda b, j, pt: (b, 0, 0)
    const = lambda b, j, pt: (0, 0)
    grid_spec = pltpu.PrefetchScalarGridSpec(
        num_scalar_prefetch=1,
        grid=(bs, n_pages_total // pc),
        in_specs=[pl.BlockSpec(lam_p.shape, const), pl.BlockSpec((1, dv), const),
                  pl.BlockSpec((None, tq, w), per_b),
                  pl.BlockSpec((None,) + k_new.shape[1:], per_b),
                  pl.BlockSpec((None,) + v_new.shape[1:], per_b)]
        + _page_specs(page_rows, pc) + _page_specs(page_rows, pc),
        out_specs=pl.BlockSpec((None, tq, w), per_b),
        scratch_shapes=[pltpu.VMEM((rows, dv), BF16), pltpu.VMEM((rows, page_rows), F32),
                        pltpu.VMEM((rows, 1), F32), pltpu.VMEM((rows, 1), F32), pltpu.VMEM((rows, dv), F32)])
    return pl.pallas_call(
        functools.partial(_diff_decode_body, n_pages=pc, tq=tq, lam_init=lam_init),
        grid_spec=grid_spec,
        out_shape=jax.ShapeDtypeStruct((bs, tq, w), F32),
        compiler_params=_cparams(("parallel", "arbitrary"), 48),
        name="diff_decode",
    )(page_table, lam_p, subln_g, q, k_new, v_new, *([cache_k] * pc), *([cache_v] * pc))


def _chunk_perm(page_tokens):
    n = page_tokens * G_B
    chunks = page_tokens // CMP_STRIDE
    out = np.arange(n)
    l, g, c = out // (G_B * chunks), (out // chunks) % G_B, out % chunks
    src = (c * CMP_STRIDE + l) * G_B + g
    perm = np.zeros((n, n), np.float32)
    perm[out, src] = 1.0
    return jnp.asarray(perm, BF16)


def _compress_decode_body(pt_ref, perm_ref, wk_ref, wv_ref, c_ref, *rest, n_pages):
    k_pages = rest[:n_pages]
    v_pages = rest[n_pages:2 * n_pages]
    yk_ref, yv_ref, a_scr = rest[2 * n_pages:]
    chunks = k_pages[0].shape[0] // (G_B * CMP_STRIDE)
    per_g = n_pages * chunks
    for a, (pages, w_ref, y_ref) in enumerate(((k_pages, wk_ref, yk_ref), (v_pages, wv_ref, yv_ref))):
        for p in range(n_pages):
            pp = _dot(perm_ref[...], pages[p][...].astype(BF16))
            for l in range(CMP_STRIDE):
                for g in range(G_B):
                    src = (l * G_B + g) * chunks
                    dst = g * per_g + p * chunks
                    a_scr[dst:dst + chunks, l * DH_B:(l + 1) * DH_B] = pp[src:src + chunks]
        y = _dot(a_scr[...].astype(BF16), w_ref[...])
        y = jnp.concatenate([y[:, :DH_B] + c_ref[a, 0:1, :], y[:, DH_B:]], axis=1)
        for g in range(G_B):
            y_ref[g] = y[g * per_g:(g + 1) * per_g]


def _compress_decode(page_table, perm, wk, wv, const, cache_k, cache_v):
    bs, n_pages_total = page_table.shape
    pc = GROUP_PAGES_PER_STEP
    page_rows = cache_k.shape[1]
    chunks = page_rows // (G_B * CMP_STRIDE)
    n_chunk = n_pages_total * chunks
    const2 = lambda b, j, pt: (0, 0)
    y_spec = pl.BlockSpec((None, G_B, pc * chunks, 2 * DH_B), lambda b, j, pt: (b, 0, j, 0))
    y_shape = jax.ShapeDtypeStruct((bs, G_B, n_chunk, 2 * DH_B), F32)
    grid_spec = pltpu.PrefetchScalarGridSpec(
        num_scalar_prefetch=1,
        grid=(bs, n_pages_total // pc),
        in_specs=[pl.BlockSpec(perm.shape, const2), pl.BlockSpec(wk.shape, const2), pl.BlockSpec(wv.shape, const2),
                  pl.BlockSpec(const.shape, lambda b, j, pt: (0, 0, 0))]
        + _page_specs(page_rows, pc) + _page_specs(page_rows, pc),
        out_specs=[y_spec, y_spec],
        scratch_shapes=[pltpu.VMEM((G_B * pc * chunks, CMP_STRIDE * DH_B), F32)])
    return pl.pallas_call(
        functools.partial(_compress_decode_body, n_pages=pc),
        grid_spec=grid_spec,
        out_shape=[y_shape, y_shape],
        compiler_params=_cparams(("parallel", "arbitrary"), 40),
        name="compress_decode",
    )(page_table, perm, wk, wv, const, *([cache_k] * pc), *([cache_v] * pc))


def _nsa_decode_body(pt_ref, q_ref, yk_ref, yv_ref, selmap_ref, expand_ref, gate_ref, ksn_ref, vsn_ref, kwn_ref,
                     vwn_ref, wk_ref, wv_ref, *rest, n_pages, tq, past_len, n_sel, n_top):
    k_pages = rest[:n_pages]
    v_pages = rest[n_pages:2 * n_pages]
    o_ref, wko_ref, wvo_ref, q_scr, sel_scr, bias_scr, oc_scr, m_scr, l_scr, acc_scr = rest[2 * n_pages:]
    j = pl.program_id(1)
    rows = H_B * tq
    rows_g = HPG_B * tq
    n_keys = k_pages[0].shape[0]
    log_tq = int(math.log2(tq))
    log_g = int(math.log2(G_B))

    def group_of_row(shape):
        return lax.shift_right_logical(lax.broadcasted_iota(jnp.int32, shape, 0), int(math.log2(rows_g)))

    def qpos_of_row(shape):
        return past_len + (lax.broadcasted_iota(jnp.int32, shape, 0) & (tq - 1))

    @pl.when(j == 0)
    def _():
        q = q_ref[...]
        q_scr[...] = jnp.concatenate([q[:, h * DH_B:(h + 1) * DH_B] for h in range(H_B)], axis=0).astype(BF16)
        qpos_col = past_len + lax.broadcasted_iota(jnp.int32, (tq, 1), 0)
        for g in range(G_B):
            qg = q_scr[g * rows_g:(g + 1) * rows_g, :]
            kcmp = _compressed_kv(yk_ref[g])
            vcmp = _compressed_kv(yv_ref[g])
            n_cmp = kcmp.shape[0]
            s = _dot_nt(qg, kcmp)
            cmp_end = lax.broadcasted_iota(jnp.int32, (rows_g, n_cmp), 1) * CMP_STRIDE + (CMP_BLOCK - 1)
            p = _masked_softmax(s, cmp_end <= qpos_of_row((rows_g, n_cmp)))
            oc_scr[g * rows_g:(g + 1) * rows_g, :] = _dot(p.astype(BF16), vcmp)
            p_sum = p[0:tq]
            for h in range(1, HPG_B):
                p_sum = p_sum + p[h * tq:(h + 1) * tq]
            imp = jnp.dot(p_sum, selmap_ref[...], precision=lax.Precision.HIGHEST, preferred_element_type=F32)
            sel = _select_blocks(imp, qpos_col, n_sel, n_top)
            for h in range(HPG_B):
                sel_scr[g * rows_g + h * tq:g * rows_g + (h + 1) * tq, :] = sel
        sel_rows = sel_scr[...]
        n_lanes = sel_rows.shape[1]
        exp_slots = expand_ref.shape[1]
        blocks_per_exp = exp_slots // (SEL_BLOCK * G_B)
        slot = lax.broadcasted_iota(jnp.int32, (rows, exp_slots), 1)
        same_group = group_of_row((rows, exp_slots)) == (slot & (G_B - 1))
        for c in range(bias_scr.shape[1] // exp_slots):
            shifted = pltpu.roll(sel_rows, (n_lanes - c * blocks_per_exp) % n_lanes, 1) if c else sel_rows
            chosen = _dot(shifted[:, :LANES].astype(BF16), expand_ref[...])
            bias_scr[:, c * exp_slots:(c + 1) * exp_slots] = jnp.where((chosen > 0.5) & same_group, 0.0, NEG)
        m_scr[...] = jnp.full(m_scr.shape, NEG, F32)
        l_scr[...] = jnp.zeros(l_scr.shape, F32)
        acc_scr[...] = jnp.zeros(acc_scr.shape, F32)

    qq = q_scr[...]
    step_slots = n_pages * n_keys
    bias = bias_scr[:, pl.ds(pl.multiple_of(j * step_slots, step_slots), step_slots)]
    ss = [_dot_nt(qq, k_pages[r][...].astype(BF16)) + bias[:, r * n_keys:(r + 1) * n_keys] for r in range(n_pages)]
    vs = [v_pages[r][...].astype(BF16) for r in range(n_pages)]
    m_scr[...], l_scr[...], acc_scr[...] = _online_update_pages(ss, vs, m_scr[...], l_scr[...], acc_scr[...])

    @pl.when(j == pl.num_programs(1) - 1)
    def _():
        sel_rows = sel_scr[...]
        sel_lane = lax.broadcasted_iota(jnp.int32, sel_rows.shape, 1)

        def block_selected(blk):
            return jnp.sum(jnp.where(sel_lane == blk, sel_rows, 0.0), axis=1, keepdims=True) > 0.5

        n_new = ksn_ref.shape[0]
        c_new = lax.broadcasted_iota(jnp.int32, (rows, n_new), 1)
        t_new = lax.shift_right_logical(c_new, log_g)
        grp_ok = group_of_row((rows, n_new)) == (c_new & (G_B - 1))
        causal = (past_len + t_new) <= qpos_of_row((rows, n_new))
        ok = grp_ok & causal & block_selected(past_len // SEL_BLOCK)
        s = jnp.where(ok, _dot_nt(qq, ksn_ref[...].astype(BF16)), NEG)
        _, l, acc = _online_update(s, vsn_ref[...].astype(BF16), m_scr[...], l_scr[...], acc_scr[...])
        o_s = acc / l

        n_win = wk_ref.shape[0]
        win_tokens = n_win // G_B
        c_w = lax.broadcasted_iota(jnp.int32, (rows, n_win), 1)
        kpos = past_len - win_tokens + lax.shift_right_logical(c_w, log_g)
        dist = qpos_of_row((rows, n_win)) - kpos
        ok_w = (group_of_row((rows, n_win)) == (c_w & (G_B - 1))) & (dist >= 0) & (dist <= WINDOW) & (kpos >= 0)
        s_w = jnp.where(ok_w, _dot_nt(qq, wk_ref[...].astype(BF16)), NEG)
        dist_n = qpos_of_row((rows, n_new)) - (past_len + t_new)
        ok_n = grp_ok & (dist_n >= 0) & (dist_n <= WINDOW)
        s_n = jnp.where(ok_n, _dot_nt(qq, kwn_ref[...].astype(BF16)), NEG)
        m = jnp.maximum(jnp.max(s_w, axis=-1, keepdims=True), jnp.max(s_n, axis=-1, keepdims=True))
        e_w = jnp.where(ok_w, jnp.exp2(s_w - m), 0.0)
        e_n = jnp.where(ok_n, jnp.exp2(s_n - m), 0.0)
        l_w = jnp.sum(e_w, axis=-1, keepdims=True) + jnp.sum(e_n, axis=-1, keepdims=True)
        o_w = (_dot(e_w.astype(BF16), wv_ref[...].astype(BF16))
               + _dot(e_n.astype(BF16), vwn_ref[...].astype(BF16))) / jnp.where(l_w > 0.0, l_w, 1.0)

        gates = gate_ref[...]
        o_c = oc_scr[...]
        for head in range(H_B):
            rs = slice(head * tq, (head + 1) * tq)
            o_ref[:, head * DH_B:(head + 1) * DH_B] = (
                gates[:, 3 * head:3 * head + 1] * o_c[rs] + gates[:, 3 * head + 1:3 * head + 2] * o_s[rs]
                + gates[:, 3 * head + 2:3 * head + 3] * o_w[rs])

        keep = n_win - n_new
        wko_ref[0:keep, :] = wk_ref[n_new:n_win, :]
        wko_ref[keep:n_win, :] = kwn_ref[...]
        wvo_ref[0:keep, :] = wv_ref[n_new:n_win, :]
        wvo_ref[keep:n_win, :] = vwn_ref[...]


def _nsa_decode(page_table, q, yk, yv, selmap, gates, ks_new, vs_new, kw_new, vw_new, win_k, win_v,
                cache_k, cache_v, past_len, n_sel):
    bs, tq, w = q.shape
    n_pages_total = page_table.shape[1]
    pc = GROUP_PAGES_PER_STEP
    rows = H_B * tq
    page_rows = cache_k.shape[1]
    exp_slots = 2048
    slots_per_block = SEL_BLOCK * G_B
    expand = jnp.asarray(np.arange(LANES)[:, None] == (np.arange(exp_slots)[None, :] // slots_per_block), BF16)
    assert (n_pages_total * page_rows) % exp_slots == 0 and exp_slots // slots_per_block <= LANES
    per_b3 = lambda b, j, pt: (b, 0, 0)
    per_b4 = lambda b, j, pt: (b, 0, 0, 0)
    const2 = lambda b, j, pt: (0, 0)
    blk3 = lambda a: pl.BlockSpec((None,) + a.shape[1:], per_b3)
    y_spec = pl.BlockSpec((None,) + yk.shape[1:], per_b4)
    grid_spec = pltpu.PrefetchScalarGridSpec(
        num_scalar_prefetch=1,
        grid=(bs, n_pages_total // pc),
        in_specs=[blk3(q), y_spec, y_spec, pl.BlockSpec(selmap.shape, const2), pl.BlockSpec(expand.shape, const2),
                  blk3(gates), blk3(ks_new), blk3(vs_new), blk3(kw_new), blk3(vw_new), blk3(win_k), blk3(win_v)]
        + _page_specs(page_rows, pc) + _page_specs(page_rows, pc),
        out_specs=[blk3(q), blk3(win_k), blk3(win_v)],
        scratch_shapes=[pltpu.VMEM((rows, DH_B), BF16), pltpu.VMEM((rows, selmap.shape[1]), F32),
                        pltpu.VMEM((rows, n_pages_total * page_rows), F32),
                        pltpu.VMEM((rows, DH_B), F32), pltpu.VMEM((rows, 1), F32), pltpu.VMEM((rows, 1), F32),
                        pltpu.VMEM((rows, DH_B), F32)])
    return pl.pallas_call(
        functools.partial(_nsa_decode_body, n_pages=pc, tq=tq, past_len=past_len, n_sel=n_sel,
                          n_top=min(SEL_TOP_N, n_sel)),
        grid_spec=grid_spec,
        out_shape=[jax.ShapeDtypeStruct((bs, tq, w), F32), jax.ShapeDtypeStruct(win_k.shape, F32),
                   jax.ShapeDtypeStruct(win_v.shape, F32)],
        compiler_params=_cparams(("parallel", "arbitrary"), 48),
        name="nsa_decode",
    )(page_table, q, yk, yv, selmap, expand, gates, ks_new, vs_new, kw_new, vw_new, win_k, win_v,
      *([cache_k] * pc), *([cache_v] * pc))


def kernel(x_prompt, x_sample, cache_diff_k, cache_diff_v, cache_cmp_k, cache_cmp_v, cache_sel_k, cache_sel_v,
           cache_win_k, cache_win_v, page_table, c_prompt, c_sample, w_mod, b_mod, norm_g, w_in, w_out, lam_p,
           subln_g, cmp_pe, cmp_phi, ffn_gate, ffn_up, ffn_down):
    depth = w_mod.shape[0]
    assert depth == 1, "single-layer step"
    B, T, D = x_prompt.shape
    Bs, Ts, _ = x_sample.shape
    n_pool, page = cache_diff_k.shape[1], cache_diff_k.shape[2]
    past_len = page_table.shape[1] * page
    win_len = cache_win_k.shape[2]
    assert T % 512 == 0 and win_len <= T and win_len == WINDOW and past_len >= win_len
    assert past_len % SEL_BLOCK == 0 and Ts <= SEL_BLOCK and (Ts & (Ts - 1)) == 0
    assert (past_len + Ts - CMP_BLOCK) // CMP_STRIDE + 1 <= past_len // CMP_STRIDE
    lam_init = 0.8 - 0.6 * math.exp(-0.3 * 0)
    Np, Ns = B * T, Bs * Ts
    gq = G_B * DH_B

    wg = [ffn_gate[0, i].astype(BF16) for i in range(2)]
    wu = [ffn_up[0, i].astype(BF16) for i in range(2)]
    wd = [ffn_down[0, i].astype(BF16) for i in range(2)]
    w_pad = jnp.pad(w_in[0].astype(BF16), ((0, 0), (0, N_IN_BLOCKS * IN_BLOCK - w_in.shape[2])))
    w_out_a, w_out_b = w_out[0, :H_A * 2 * DH_A].astype(BF16), w_out[0, H_A * 2 * DH_A:].astype(BF16)
    ng = norm_g[0].reshape(6, 1, D)
    wck, pek = _compress_weights(cmp_pe[0, 0], cmp_phi[0, 0])
    wcv, pev = _compress_weights(cmp_pe[0, 1], cmp_phi[0, 1])
    cmp_const = _cmp_const(pek, pev, wck, wcv)

    n_c = B + Bs
    c_all = jnp.pad(jnp.concatenate([c_prompt, c_sample], axis=0), ((0, -n_c % 8), (0, 0)))
    mod = _adaln_mod(c_all, w_mod[0], b_mod[0][None]).reshape(c_all.shape[0], 9, D)
    mod_p = mod[:B].transpose(1, 0, 2).reshape(9, B, 1, D)
    mod_s = jnp.repeat(mod[B:n_c].transpose(1, 0, 2), Ts, axis=1)

    def split_states(pr, batch, seq):
        return (pr["k_a"].reshape(1, batch, seq, H_A, 2 * DH_A), pr["v_a"].reshape(1, batch, seq, H_A, 2 * DH_A),
                pr["kc"].reshape(1, batch, seq, G_B, DH_B), pr["vc"].reshape(1, batch, seq, G_B, DH_B),
                pr["ks"].reshape(1, batch, seq, G_B, DH_B), pr["vs"].reshape(1, batch, seq, G_B, DH_B))

    names = ("q_a", "k_a", "k_a_bf", "v_a", "v_a_bf", "q_b", "kc", "vc", "ks", "vs", "kw", "vw", "nkv", "gates")

    tm_p = 512
    xp = x_prompt.reshape(Np, D)
    x1 = _ffn(xp, mod_p, (0, 1, 2), T, ng[0], ng[1], wg[0], wu[0], wd[0], tm_p, FFN_TF)
    pr = dict(zip(names, _inproj(x1, mod_p, 3, 4, T, ng[2], w_pad, _rope_tables(np.arange(T)), tm_p)))
    o_a = _diff_prompt(pr["q_a"], pr["k_a_bf"], pr["v_a_bf"], lam_p[0], subln_g[0][None], B, T, lam_init)
    yk, yv = _compress_prompt(pr["kc"], pr["vc"], wck, wcv, cmp_const, B, T)
    n_cmp_p = (T - CMP_BLOCK) // CMP_STRIDE + 1
    n_sel_p = -(-T // SEL_BLOCK)
    assert n_sel_p <= LANES
    selmap_p = _sel_map(T // CMP_STRIDE, n_cmp_p, n_sel_p, n_sel_p).T
    o_b = _nsa_prompt(pr["q_b"], yk, yv, pr["nkv"], pr["gates"], selmap_p, B, T)
    x2 = _outproj(o_a, o_b, w_out_a, w_out_b, x1, mod_p, 5, T, ng[3], tm_p)
    y_p = _ffn(x2, mod_p, (6, 7, 8), T, ng[4], ng[5], wg[1], wu[1], wd[1], tm_p, FFN_TF).reshape(B, T, D)
    st_p = split_states(pr, B, T)
    wk_p = pr["kw"].reshape(1, B, T, G_B, DH_B)[:, :, T - win_len:]
    wv_p = pr["vw"].reshape(1, B, T, G_B, DH_B)[:, :, T - win_len:]

    xs = x_sample.reshape(Ns, D)
    x1s = _ffn(xs, mod_s, (0, 1, 2), 1, ng[0], ng[1], wg[0], wu[0], wd[0], Ns, FFN_TF)
    pos_s = np.tile(past_len + np.arange(Ts), Bs)
    ps = dict(zip(names, _inproj(x1s, mod_s, 3, 4, 1, ng[2], w_pad, _rope_tables(pos_s), Ns)))
    ck = cache_diff_k[0].reshape(n_pool, page * H_A, 2 * DH_A)
    cv = cache_diff_v[0].reshape(n_pool, page * H_A, 2 * DH_A)
    o_as = _diff_decode(page_table, lam_p[0], subln_g[0][None],
                        ps["q_a"].astype(F32).reshape(Bs, Ts, H_A * 2 * DH_A),
                        ps["k_a"].reshape(Bs, Ts * H_A, 2 * DH_A), ps["v_a"].reshape(Bs, Ts * H_A, 2 * DH_A),
                        ck, cv, lam_init)
    pool3 = lambda c: c[0].reshape(n_pool, page * G_B, DH_B)
    yks, yvs = _compress_decode(page_table, _chunk_perm(page), wck, wcv, cmp_const,
                                pool3(cache_cmp_k), pool3(cache_cmp_v))
    tk_s = past_len + Ts
    n_cmp_s = (tk_s - CMP_BLOCK) // CMP_STRIDE + 1
    n_sel_s = -(-tk_s // SEL_BLOCK)
    selmap_s = _sel_map(past_len // CMP_STRIDE, n_cmp_s, n_sel_s, -(-n_sel_s // LANES) * LANES)
    new3 = lambda a: a.reshape(Bs, Ts * G_B, DH_B)
    o_bs, wk_s, wv_s = _nsa_decode(
        page_table, ps["q_b"].astype(F32).reshape(Bs, Ts, H_B * DH_B), yks, yvs, selmap_s,
        ps["gates"].reshape(Bs, Ts, LANES), new3(ps["ks"]), new3(ps["vs"]), new3(ps["kw"]), new3(ps["vw"]),
        cache_win_k[0].reshape(Bs, win_len * G_B, DH_B), cache_win_v[0].reshape(Bs, win_len * G_B, DH_B),
        pool3(cache_sel_k), pool3(cache_sel_v), past_len, n_sel_s)
    x2s = _outproj(o_as.reshape(Ns, -1).astype(BF16), o_bs.reshape(Ns, -1).astype(BF16), w_out_a, w_out_b,
                   x1s, mod_s, 5, 1, ng[3], Ns)
    y_s = _ffn(x2s, mod_s, (6, 7, 8), 1, ng[4], ng[5], wg[1], wu[1], wd[1], Ns, FFN_TF).reshape(Bs, Ts, D)
    st_s = split_states(ps, Bs, Ts)
    wk_s = wk_s.reshape(1, Bs, win_len, G_B, DH_B)
    wv_s = wv_s.reshape(1, Bs, win_len, G_B, DH_B)

    return (y_p, y_s) + st_p + (wk_p, wv_p) + st_s + (wk_s, wv_s)
```

```python
import functools
import math

import numpy as np
import jax
import jax.numpy as jnp
from jax import lax
from jax.experimental import pallas as pl
from jax.experimental.pallas import tpu as pltpu

F32 = jnp.float32
BF16 = jnp.bfloat16

DH_A = 64
H_A = 8
DH_B = 128
H_B = 8
G_B = 2
HPG_B = H_B // G_B
CMP_BLOCK = 32
CMP_STRIDE = 16
SEL_BLOCK = 64
SEL_TOP_N = 16
N_LOCAL = 2
WINDOW = 512
ROPE_THETA = 10000.0
EPS = 1e-6
NEG = -1e30
FORCE = 1e4
LOG2E = math.log2(math.e)

LANES = 128
MIB = 1024 * 1024
PAGES_PER_STEP = 8
GROUP_PAGES_PER_STEP = 16
FFN_TF = 1024


def _dot(a, b):
    return jnp.dot(a, b, preferred_element_type=F32)


def _dot_nt(a, b):
    return lax.dot_general(a, b, (((1,), (1,)), ((), ())), preferred_element_type=F32)


def _rms(x, g):
    return x * lax.rsqrt(jnp.mean(x * x, axis=-1, keepdims=True) + EPS) * g


def _cparams(sem, vmem_mib):
    return pltpu.CompilerParams(dimension_semantics=sem, vmem_limit_bytes=vmem_mib * MIB)


def _mod_body(c_ref, w_ref, b_ref, o_ref):
    c = c_ref[...]
    a = (c * jax.nn.sigmoid(c)).astype(BF16)
    o_ref[...] = _dot(a, w_ref[...].astype(BF16)) + b_ref[...]


def _adaln_mod(c, w_mod, b_mod):
    R, D = c.shape
    n_out = w_mod.shape[1]
    tn = 1024
    return pl.pallas_call(
        _mod_body,
        grid=(n_out // tn,),
        in_specs=[pl.BlockSpec((R, D), lambda j: (0, 0)),
                  pl.BlockSpec((D, tn), lambda j: (0, j)),
                  pl.BlockSpec((1, tn), lambda j: (0, j))],
        out_specs=pl.BlockSpec((R, tn), lambda j: (0, j)),
        out_shape=jax.ShapeDtypeStruct((R, n_out), F32),
        compiler_params=_cparams(("arbitrary",), 40),
        name="adaln_mod",
    )(c, w_mod, b_mod)


def _mod_spec(mod, k, tm, rows_per_group):
    if mod.ndim == 4:
        return pl.BlockSpec((None, None, 1, mod.shape[-1]),
                            lambda i, *_, k=k: (k, (i * tm) // rows_per_group, 0, 0))
    return pl.BlockSpec((None, tm, mod.shape[-1]), lambda i, *_, k=k: (k, i, 0))


def _ffn_body(x_ref, sh_ref, sc_ref, gt_ref, gpre_ref, gpost_ref, wg_ref, wu_ref, wd_ref, o_ref,
              h_scr, acc_scr, *, n_chunk):
    f = pl.program_id(1)

    @pl.when(f == 0)
    def _():
        h = _rms(x_ref[...], gpre_ref[...]) * (1.0 + sc_ref[...]) + sh_ref[...]
        h_scr[...] = h.astype(BF16)

    h = h_scr[...]
    g = _dot(h, wg_ref[...])
    u = _dot(h, wu_ref[...])
    a = (g * jax.nn.sigmoid(g) * u).astype(BF16)
    d = acc_scr.shape[1]
    cw = d // n_chunk

    @pl.when(f == 0)
    def _():
        for n in range(n_chunk):
            acc_scr[:, n * cw:(n + 1) * cw] = _dot(a, wd_ref[:, n * cw:(n + 1) * cw])

    @pl.when(f > 0)
    def _():
        for n in range(n_chunk):
            acc_scr[:, n * cw:(n + 1) * cw] += _dot(a, wd_ref[:, n * cw:(n + 1) * cw])

    @pl.when(f == pl.num_programs(1) - 1)
    def _():
        o_ref[...] = x_ref[...] + 0.5 * gt_ref[...] * _rms(acc_scr[...], gpost_ref[...])


def _ffn(x, mod, ks, rows_per_group, g_pre, g_post, wg, wu, wd, layer, tm, tf):
    n, d = x.shape
    dff = wg.shape[2]
    row = lambda i, f: (i, 0)
    const = lambda i, f: (0, 0)
    return pl.pallas_call(
        functools.partial(_ffn_body, n_chunk=4),
        grid=(n // tm, dff // tf),
        in_specs=[pl.BlockSpec((tm, d), row),
                  _mod_spec(mod, ks[0], tm, rows_per_group),
                  _mod_spec(mod, ks[1], tm, rows_per_group),
                  _mod_spec(mod, ks[2], tm, rows_per_group),
                  pl.BlockSpec((1, d), const), pl.BlockSpec((1, d), const),
                  pl.BlockSpec((None, d, tf), lambda i, f: (layer, 0, f)),
                  pl.BlockSpec((None, d, tf), lambda i, f: (layer, 0, f)),
                  pl.BlockSpec((None, tf, d), lambda i, f: (layer, f, 0))],
        out_specs=pl.BlockSpec((tm, d), row),
        out_shape=jax.ShapeDtypeStruct((n, d), F32),
        scratch_shapes=[pltpu.VMEM((tm, d), BF16), pltpu.VMEM((tm, d), F32)],
        compiler_params=_cparams(("parallel", "arbitrary"), 60),
        name="ffn",
    )(x, mod, mod, mod, g_pre, g_post, wg, wu, wd)


IN_BLOCK = 1024
N_IN_BLOCKS = 6


def _inproj_body(x_ref, sh_ref, sc_ref, g_ref, w_ref, ca_ref, sa_ref, cb_ref, sb_ref,
                 qa_ref, ka_ref, kab_ref, va_ref, vab_ref, qb_ref, kc_ref, vc_ref, ks_ref, vs_ref,
                 kw_ref, vw_ref, nkv_ref, gate_ref, h_scr):
    j = pl.program_id(1)

    @pl.when(j == 0)
    def _():
        h = _rms(x_ref[...], g_ref[...]) * (1.0 + sc_ref[...]) + sh_ref[...]
        h_scr[...] = h.astype(BF16)

    lane = lax.broadcasted_iota(jnp.int32, (1, LANES), 1)
    low32 = (lane & (DH_A - 1)) < (DH_A // 2)

    def cols(c):
        y = _dot(h_scr[...], w_ref[:, c * 256:(c + 1) * 256])
        return y[:, :LANES], y[:, LANES:]

    def rope64(y):
        partner = jnp.where(low32, pltpu.roll(y, LANES - DH_A // 2, 1), pltpu.roll(y, DH_A // 2, 1))
        return y * ca_ref[...] + partner * sa_ref[...]

    def rope128(y):
        return y * cb_ref[...] + pltpu.roll(y, DH_B // 2, 1) * sb_ref[...]

    def tile(t):
        return slice(t * LANES, (t + 1) * LANES)

    @pl.when(j == 0)
    def _():
        for c in range(4):
            for t, y in enumerate(cols(c)):
                qa_ref[:, tile(2 * c + t)] = (rope64(y) * (DH_A ** -0.5 * LOG2E)).astype(BF16)

    @pl.when(j == 1)
    def _():
        for c in range(4):
            for t, y in enumerate(cols(c)):
                r = rope64(y)
                ka_ref[:, tile(2 * c + t)] = r
                kab_ref[:, tile(2 * c + t)] = r.astype(BF16)

    @pl.when(j == 2)
    def _():
        for c in range(4):
            for t, y in enumerate(cols(c)):
                va_ref[:, tile(2 * c + t)] = y
                vab_ref[:, tile(2 * c + t)] = y.astype(BF16)

    @pl.when(j == 3)
    def _():
        for c in range(4):
            for t, y in enumerate(cols(c)):
                qb_ref[:, tile(2 * c + t)] = (rope128(y) * (DH_B ** -0.5 * LOG2E)).astype(BF16)

    tm = x_ref.shape[0]

    def group_rows(g):
        return pl.ds(g, tm, stride=G_B)

    @pl.when(j == 4)
    def _():
        for c, (ref, roped) in enumerate(((kc_ref, True), (vc_ref, False), (ks_ref, True), (vs_ref, False))):
            for t, y in enumerate(cols(c)):
                r = rope128(y) if roped else y
                ref[group_rows(t), :] = r
                if c >= 2:
                    nkv_ref[:, tile(2 * (c - 2) + t)] = r.astype(BF16)

    @pl.when(j == 5)
    def _():
        for c, (ref, roped) in enumerate(((kw_ref, True), (vw_ref, False))):
            for t, y in enumerate(cols(c)):
                r = rope128(y) if roped else y
                ref[group_rows(t), :] = r
                nkv_ref[:, tile(4 + 2 * c + t)] = r.astype(BF16)
        gate_ref[...] = jax.nn.sigmoid(cols(2)[0])


def _inproj(x, mod, k_shift, k_scale, rows_per_group, g, w_pad, tabs, tm):
    n, d = x.shape
    gq = G_B * DH_B
    n_tab = tabs[0].shape[0] // tm
    row = lambda i, j: (i, 0)
    const = lambda i, j: (0, 0)
    tab = pl.BlockSpec((tm, LANES), lambda i, j: (i % n_tab, 0))
    wide = lambda dt: jax.ShapeDtypeStruct((n, IN_BLOCK), dt)
    grouped = jax.ShapeDtypeStruct((n * G_B, DH_B), F32)
    out_shape = [wide(BF16), wide(F32), wide(BF16), wide(F32), wide(BF16), wide(BF16)] + [grouped] * 6 + [
        wide(BF16), jax.ShapeDtypeStruct((n, LANES), F32)]
    out_specs = [pl.BlockSpec((tm * s.shape[0] // n, s.shape[1]), row) for s in out_shape]
    return pl.pallas_call(
        _inproj_body,
        grid=(n // tm, N_IN_BLOCKS),
        in_specs=[pl.BlockSpec((tm, d), row),
                  _mod_spec(mod, k_shift, tm, rows_per_group),
                  _mod_spec(mod, k_scale, tm, rows_per_group),
                  pl.BlockSpec((1, d), const),
                  pl.BlockSpec((d, IN_BLOCK), lambda i, j: (0, j)),
                  tab, tab, tab, tab],
        out_specs=out_specs,
        out_shape=out_shape,
        scratch_shapes=[pltpu.VMEM((tm, d), BF16)],
        compiler_params=_cparams(("parallel", "arbitrary"), 56),
        name="inproj",
    )(x, mod, mod, g, w_pad, *tabs)


def _rope_tables(pos):
    pos = np.asarray(pos, np.float64)[:, None]

    def tables(head_dim):
        half = head_dim // 2
        inv = ROPE_THETA ** (-np.arange(half, dtype=np.float64) / half)
        ang = pos * inv[None, :]
        cos = np.concatenate([np.cos(ang), np.cos(ang)], axis=1)
        sin = np.concatenate([-np.sin(ang), np.sin(ang)], axis=1)
        rep = LANES // head_dim
        return (jnp.asarray(np.tile(cos, (1, rep)), F32), jnp.asarray(np.tile(sin, (1, rep)), F32))

    ca, sa = tables(DH_A)
    cb, sb = tables(DH_B)
    return ca, sa, cb, sb


def _lambda(lamp_ref, lam_init):
    lp = lamp_ref[...]
    a = jnp.sum(lp[0:1] * lp[1:2], axis=1, keepdims=True)
    b = jnp.sum(lp[2:3] * lp[3:4], axis=1, keepdims=True)
    return jnp.exp(a) - jnp.exp(b) + lam_init


def _online_update(s, v, m, l, acc):
    m_new = jnp.maximum(m, jnp.max(s, axis=-1, keepdims=True))
    p = jnp.exp2(s - m_new)
    alpha = jnp.exp2(m - m_new)
    l = alpha * l + jnp.sum(p, axis=-1, keepdims=True)
    acc = alpha * acc + _dot(p.astype(BF16), v)
    return m_new, l, acc


def _online_update_pages(ss, vs, m, l, acc):
    m_blk = jnp.max(ss[0], axis=-1, keepdims=True)
    for s in ss[1:]:
        m_blk = jnp.maximum(m_blk, jnp.max(s, axis=-1, keepdims=True))
    m_new = jnp.maximum(m, m_blk)
    alpha = jnp.exp2(m - m_new)
    l = alpha * l
    acc = alpha * acc
    for s, v in zip(ss, vs):
        p = jnp.exp2(s - m_new)
        l = l + jnp.sum(p, axis=-1, keepdims=True)
        acc = acc + _dot(p.astype(BF16), v)
    return m_new, l, acc


def _softmax_init(rows, dv):
    return (jnp.full((rows, 1), NEG, F32), jnp.zeros((rows, 1), F32), jnp.zeros((rows, dv), F32))


def _masked_softmax(s, mask):
    sm = jnp.where(mask, s, NEG)
    m = jnp.max(sm, axis=-1, keepdims=True)
    e = jnp.where(mask, jnp.exp2(sm - m), 0.0)
    l = jnp.sum(e, axis=-1, keepdims=True)
    return e / jnp.where(l > 0.0, l, 1.0)


def _select_blocks(imp, qpos, n_blocks, n_top):
    shape = imp.shape
    blk = lax.broadcasted_iota(jnp.int32, shape, 1)
    rel = lax.shift_right_logical(qpos, int(math.log2(SEL_BLOCK))) - blk
    visible = rel >= 0
    forced = (blk == 0) | (visible & (rel < N_LOCAL))
    score = jnp.where(visible, jnp.where(forced, FORCE + imp, imp), NEG)
    rank = jnp.zeros(shape, F32)
    for i in range(n_blocks):
        col = score[:, i:i + 1]
        beats = (col > score) | ((col == score) & (blk > i))
        rank = rank + jnp.where(beats, 1.0, 0.0)
    return jnp.where((rank < n_top) & visible, 1.0, 0.0)


def _select_blocks_t(imp_t, qpos_row, n_blocks, n_top):
    shape = imp_t.shape
    blk = lax.broadcasted_iota(jnp.int32, shape, 0)
    rel = lax.shift_right_logical(qpos_row, int(math.log2(SEL_BLOCK))) - blk
    visible = rel >= 0
    forced = (blk == 0) | (visible & (rel < N_LOCAL))
    score = jnp.where(visible, jnp.where(forced, FORCE + imp_t, imp_t), NEG)
    rank = jnp.zeros(shape, F32)
    for i in range(n_blocks):
        row = score[i:i + 1, :]
        beats = (row > score) | ((row == score) & (blk > i))
        rank = rank + jnp.where(beats, 1.0, 0.0)
    return jnp.where((rank < n_top) & visible, 1.0, 0.0)


def _compressed_kv(y):
    n = y.shape[0]
    return (y[:, :DH_B] + pltpu.roll(y[:, DH_B:], n - 1, 0)).astype(BF16)


def _diff_prompt_body(lamp_ref, g_ref, q_ref, k_ref, v_ref, o_ref, *, tq, lam_init):
    qi = pl.program_id(2)
    lam = _lambda(lamp_ref, lam_init)
    q = q_ref[...]
    lane = lax.broadcasted_iota(jnp.int32, (1, 2 * DH_A), 1)
    zero = jnp.zeros_like(q)
    qq = jnp.concatenate([jnp.where(lane < DH_A, q, zero), jnp.where(lane >= DH_A, q, zero)], axis=0)

    def chunk(j):
        start = pl.multiple_of(j * tq, tq)
        return k_ref[pl.ds(start, tq), :], v_ref[pl.ds(start, tq), :]

    def body(j, carry):
        k, v = chunk(j)
        return _online_update(_dot_nt(qq, k), v, *carry)

    carry = lax.fori_loop(0, qi, body, _softmax_init(2 * tq, 2 * DH_A))
    k, v = chunk(qi)
    s = _dot_nt(qq, k)
    r = lax.broadcasted_iota(jnp.int32, s.shape, 0) & (tq - 1)
    c = lax.broadcasted_iota(jnp.int32, s.shape, 1)
    m, l, acc = _online_update(jnp.where(c <= r, s, NEG), v, *carry)
    o = acc / l
    o_ref[...] = (_rms(o[:tq] - lam * o[tq:], g_ref[...]) * (1.0 - lam_init)).astype(o_ref.dtype)


def _diff_prompt(q, k, v, lam_p, subln_g, batch, seq, lam_init, tq=512):
    n, w = q.shape
    dv = 2 * DH_A
    nq = seq // tq
    return pl.pallas_call(
        functools.partial(_diff_prompt_body, tq=tq, lam_init=lam_init),
        grid=(batch, H_A, nq),
        in_specs=[pl.BlockSpec(lam_p.shape, lambda b, h, i: (0, 0)),
                  pl.BlockSpec((1, dv), lambda b, h, i: (0, 0)),
                  pl.BlockSpec((tq, dv), lambda b, h, i: (b * nq + i, h)),
                  pl.BlockSpec((seq, dv), lambda b, h, i: (b, h)),
                  pl.BlockSpec((seq, dv), lambda b, h, i: (b, h))],
        out_specs=pl.BlockSpec((tq, dv), lambda b, h, i: (b * nq + i, h)),
        out_shape=jax.ShapeDtypeStruct((n, w), BF16),
        compiler_params=_cparams(("parallel", "parallel", "arbitrary"), 48),
        name="diff_prompt",
    )(lam_p, subln_g, q, k, v)


def _cmp_const_body(pek_ref, pev_ref, wk_ref, wv_ref, o_ref):
    for a, (pe_ref, w_ref) in enumerate(((pek_ref, wk_ref), (pev_ref, wv_ref))):
        cw = _dot(pe_ref[...].astype(BF16), w_ref[...])
        o_ref[a] = jnp.broadcast_to(cw[0:1, :DH_B] + cw[1:2, DH_B:], (8, DH_B))


def _cmp_const(pek, pev, wk, wv):
    return pl.pallas_call(
        _cmp_const_body,
        out_shape=jax.ShapeDtypeStruct((2, 8, DH_B), F32),
        name="cmp_const",
    )(pek, pev, wk, wv)


def _compress_weights(pe, phi):
    half = CMP_BLOCK // 2
    w = jnp.concatenate([phi[:half].reshape(half * DH_B, DH_B), phi[half:].reshape(half * DH_B, DH_B)], axis=1)
    pe2 = jnp.zeros((8, half * DH_B), F32)
    pe2 = pe2.at[0].set(pe[:half].reshape(-1)).at[1].set(pe[half:].reshape(-1))
    return w.astype(BF16), pe2


def _compress_prompt_body(rk_ref, rv_ref, wk_ref, wv_ref, c_ref, yk_ref, yv_ref):
    half = CMP_BLOCK // 2
    for a, (r_ref, w_ref, y_ref) in enumerate(((rk_ref, wk_ref, yk_ref), (rv_ref, wv_ref, yv_ref))):
        parts = []
        for g in range(G_B):
            parts.append(jnp.concatenate(
                [r_ref[:, (l * G_B + g) * DH_B:(l * G_B + g + 1) * DH_B] for l in range(half)], axis=1))
        a_mat = jnp.concatenate(parts, axis=0).astype(BF16)
        y = _dot(a_mat, w_ref[...])
        y = jnp.concatenate([y[:, :DH_B] + c_ref[a, 0:1, :], y[:, DH_B:]], axis=1)
        n_chunk = y.shape[0] // G_B
        for g in range(G_B):
            y_ref[g] = y[g * n_chunk:(g + 1) * n_chunk]


def _compress_prompt(kc, vc, wk, wv, const, batch, seq):
    n_chunk = seq // CMP_STRIDE
    width = CMP_STRIDE * G_B * DH_B
    rk = kc.reshape(batch, n_chunk, width)
    rv = vc.reshape(batch, n_chunk, width)
    r_spec = pl.BlockSpec((None, n_chunk, width), lambda b: (b, 0, 0))
    w_spec = pl.BlockSpec(wk.shape, lambda b: (0, 0))
    y_spec = pl.BlockSpec((None, G_B, n_chunk, 2 * DH_B), lambda b: (b, 0, 0, 0))
    y_shape = jax.ShapeDtypeStruct((batch, G_B, n_chunk, 2 * DH_B), F32)
    return pl.pallas_call(
        _compress_prompt_body,
        grid=(batch,),
        in_specs=[r_spec, r_spec, w_spec, w_spec, pl.BlockSpec(const.shape, lambda b: (0, 0, 0))],
        out_specs=[y_spec, y_spec],
        out_shape=[y_shape, y_shape],
        compiler_params=_cparams(("parallel",), 40),
        name="compress_prompt",
    )(rk, rv, wk, wv, const)


def _nsa_prompt_body(q_ref, yk_ref, yv_ref, kv_ref, gate_ref, selmap_ref, o_ref, *, tq, n_sel, n_top):
    qi = pl.program_id(1)
    rows = HPG_B * tq
    gq = G_B * DH_B
    q0 = qi * tq
    qpos_col = q0 + lax.broadcasted_iota(jnp.int32, (tq, 1), 0)
    qpos_row = q0 + lax.broadcasted_iota(jnp.int32, (1, tq), 1)

    def stack_rows(x):
        return jnp.concatenate([x] * HPG_B, axis=0)

    gates = gate_ref[...]
    for g in range(G_B):
        qg = jnp.concatenate([q_ref[:, (g * HPG_B + h) * DH_B:(g * HPG_B + h + 1) * DH_B]
                              for h in range(HPG_B)], axis=0)

        kcmp = _compressed_kv(yk_ref[g])
        vcmp = _compressed_kv(yv_ref[g])
        n_cmp = kcmp.shape[0]
        s = _dot_nt(qg, kcmp)
        cmp_end = lax.broadcasted_iota(jnp.int32, (tq, n_cmp), 1) * CMP_STRIDE + (CMP_BLOCK - 1)
        p = _masked_softmax(s, stack_rows(jnp.where(cmp_end <= qpos_col, 1.0, 0.0)) > 0.5)
        o_c = _dot(p.astype(BF16), vcmp)
        p_sum = p[0:tq]
        for h in range(1, HPG_B):
            p_sum = p_sum + p[h * tq:(h + 1) * tq]
        imp_t = lax.dot_general(selmap_ref[...], p_sum, (((1,), (1,)), ((), ())),
                                precision=lax.Precision.HIGHEST, preferred_element_type=F32)
        sel_t = _select_blocks_t(imp_t, qpos_row, n_sel, n_top)
        sel = jnp.concatenate([sel_t, jnp.zeros((LANES - n_sel, tq), F32)], axis=0).T.astype(BF16)

        blk_of_key = lax.shift_right_logical(lax.broadcasted_iota(jnp.int32, (LANES, tq), 1),
                                             int(math.log2(SEL_BLOCK)))
        blk_row = lax.broadcasted_iota(jnp.int32, (LANES, tq), 0)
        key_col = lax.broadcasted_iota(jnp.int32, (tq, tq), 1)

        def sel_step(j, carry):
            start = pl.multiple_of(j * tq, tq)
            k = kv_ref[pl.ds(start, tq), g * DH_B:(g + 1) * DH_B]
            v = kv_ref[pl.ds(start, tq), gq + g * DH_B:gq + (g + 1) * DH_B]
            expand = jnp.where(blk_row == blk_of_key + j * (tq // SEL_BLOCK), 1.0, 0.0).astype(BF16)
            ok = (_dot(sel, expand) > 0.5) & (key_col + j * tq <= qpos_col)
            bias = stack_rows(jnp.where(ok, 0.0, NEG))
            return _online_update(_dot_nt(qg, k) + bias, v, *carry)

        _, l, acc = lax.fori_loop(0, qi + 1, sel_step, _softmax_init(rows, DH_B))
        o_s = acc / l

        def win_step(dj, carry):
            j = qi - dj
            start = pl.multiple_of(j * tq, tq)
            k = kv_ref[pl.ds(start, tq), 2 * gq + g * DH_B:2 * gq + (g + 1) * DH_B]
            v = kv_ref[pl.ds(start, tq), 3 * gq + g * DH_B:3 * gq + (g + 1) * DH_B]
            dist = qpos_col - (key_col + j * tq)
            bias = stack_rows(jnp.where((dist >= 0) & (dist <= WINDOW), 0.0, NEG))
            return _online_update(_dot_nt(qg, k) + bias, v, *carry)

        n_back = WINDOW // tq
        _, l, acc = lax.fori_loop(0, jnp.minimum(qi, n_back) + 1, win_step, _softmax_init(rows, DH_B))
        o_w = acc / l

        for h in range(HPG_B):
            head = g * HPG_B + h
            rs = slice(h * tq, (h + 1) * tq)
            o = (gates[:, 3 * head:3 * head + 1] * o_c[rs] + gates[:, 3 * head + 1:3 * head + 2] * o_s[rs]
                 + gates[:, 3 * head + 2:3 * head + 3] * o_w[rs])
            o_ref[:, head * DH_B:(head + 1) * DH_B] = o.astype(o_ref.dtype)


def _nsa_prompt(q, yk, yv, nkv, gates, selmap, batch, seq, tq=256):
    n, w = q.shape
    nq = seq // tq
    n_sel = -(-seq // SEL_BLOCK)
    n_chunk = yk.shape[2]
    y_spec = pl.BlockSpec((None, G_B, n_chunk, 2 * DH_B), lambda b, i: (b, 0, 0, 0))
    return pl.pallas_call(
        functools.partial(_nsa_prompt_body, tq=tq, n_sel=n_sel, n_top=min(SEL_TOP_N, n_sel)),
        grid=(batch, nq),
        in_specs=[pl.BlockSpec((tq, w), lambda b, i: (b * nq + i, 0)),
                  y_spec, y_spec,
                  pl.BlockSpec((seq, nkv.shape[1]), lambda b, i: (b, 0)),
                  pl.BlockSpec((tq, LANES), lambda b, i: (b * nq + i, 0)),
                  pl.BlockSpec(selmap.shape, lambda b, i: (0, 0))],
        out_specs=pl.BlockSpec((tq, w), lambda b, i: (b * nq + i, 0)),
        out_shape=jax.ShapeDtypeStruct((n, w), BF16),
        compiler_params=_cparams(("parallel", "arbitrary"), 48),
        name="nsa_prompt",
    )(q, yk, yv, nkv, gates, selmap)


def _sel_map(n_chunk, n_cmp, n_sel, n_sel_pad):
    i = np.arange(n_chunk)[:, None] * CMP_STRIDE
    j = np.arange(n_sel_pad)[None, :] * SEL_BLOCK
    m = (i < j + SEL_BLOCK) & (i + CMP_BLOCK > j)
    m &= (np.arange(n_chunk)[:, None] < n_cmp) & (np.arange(n_sel_pad)[None, :] < n_sel)
    return jnp.asarray(m.astype(np.float32))


def _outproj_body(oa_ref, ob_ref, wa_ref, wb_ref, x_ref, gt_ref, g_ref, o_ref):
    m = _dot(oa_ref[...], wa_ref[...]) + _dot(ob_ref[...], wb_ref[...])
    o_ref[...] = x_ref[...] + gt_ref[...] * _rms(m, g_ref[...])


def _outproj(o_a, o_b, w_a, w_b, x, mod, k_gate, rows_per_group, g, tm):
    n, d = x.shape
    row = lambda i: (i, 0)
    const = lambda i: (0, 0)
    return pl.pallas_call(
        _outproj_body,
        grid=(n // tm,),
        in_specs=[pl.BlockSpec((tm, o_a.shape[1]), row), pl.BlockSpec((tm, o_b.shape[1]), row),
                  pl.BlockSpec(w_a.shape, const), pl.BlockSpec(w_b.shape, const),
                  pl.BlockSpec((tm, d), row),
                  _mod_spec(mod, k_gate, tm, rows_per_group),
                  pl.BlockSpec((1, d), const)],
        out_specs=pl.BlockSpec((tm, d), row),
        out_shape=jax.ShapeDtypeStruct((n, d), F32),
        compiler_params=_cparams(("parallel",), 48),
        name="outproj",
    )(o_a, o_b, w_a, w_b, x, mod, g)


def _page_specs(rows, n_pages):
    return [pl.BlockSpec((None, rows, LANES), lambda b, j, pt, r=r: (pt[b, j * n_pages + r], 0, 0))
            for r in range(n_pages)]


def _diff_decode_body(pt_ref, lamp_ref, g_ref, q_ref, knew_ref, vnew_ref, *rest, n_pages, tq, lam_init):
    k_pages = rest[:n_pages]
    v_pages = rest[n_pages:2 * n_pages]
    o_ref, q_scr, bias_scr, m_scr, l_scr, acc_scr = rest[2 * n_pages:]
    j = pl.program_id(1)
    rows = H_A * 2 * tq
    n_keys = k_pages[0].shape[0]

    @pl.when(j == 0)
    def _():
        q = q_ref[...]
        lane = lax.broadcasted_iota(jnp.int32, (1, 2 * DH_A), 1)
        parts = []
        for h in range(H_A):
            qh = q[:, h * 2 * DH_A:(h + 1) * 2 * DH_A]
            parts += [jnp.where(lane < DH_A, qh, 0.0), jnp.where(lane >= DH_A, qh, 0.0)]
        q_scr[...] = jnp.concatenate(parts, axis=0).astype(BF16)
        head_of_row = lax.shift_right_logical(lax.broadcasted_iota(jnp.int32, (rows, n_keys), 0),
                                              int(math.log2(2 * tq)))
        head_of_key = lax.broadcasted_iota(jnp.int32, (rows, n_keys), 1) & (H_A - 1)
        bias_scr[...] = jnp.where(head_of_row == head_of_key, 0.0, NEG)
        m_scr[...] = jnp.full(m_scr.shape, NEG, F32)
        l_scr[...] = jnp.zeros(l_scr.shape, F32)
        acc_scr[...] = jnp.zeros(acc_scr.shape, F32)

    qq = q_scr[...]
    ss = [_dot_nt(qq, k_pages[r][...].astype(BF16)) + bias_scr[...] for r in range(n_pages)]
    vs = [v_pages[r][...].astype(BF16) for r in range(n_pages)]
    m_scr[...], l_scr[...], acc_scr[...] = _online_update_pages(ss, vs, m_scr[...], l_scr[...], acc_scr[...])

    @pl.when(j == pl.num_programs(1) - 1)
    def _():
        n_new = knew_ref.shape[0]
        r_i = lax.broadcasted_iota(jnp.int32, (rows, n_new), 0)
        c_i = lax.broadcasted_iota(jnp.int32, (rows, n_new), 1)
        ok = ((lax.shift_right_logical(r_i, int(math.log2(2 * tq))) == (c_i & (H_A - 1)))
              & (lax.shift_right_logical(c_i, int(math.log2(H_A))) <= (r_i & (tq - 1))))
        s = jnp.where(ok, _dot_nt(qq, knew_ref[...].astype(BF16)), NEG)
        _, l, acc = _online_update(s, vnew_ref[...].astype(BF16), m_scr[...], l_scr[...], acc_scr[...])
        o = acc / l
        lam = _lambda(lamp_ref, lam_init)
        for h in range(H_A):
            base = h * 2 * tq
            d = o[base:base + tq] - lam * o[base + tq:base + 2 * tq]
            o_ref[:, h * 2 * DH_A:(h + 1) * 2 * DH_A] = _rms(d, g_ref[...]) * (1.0 - lam_init)


def _diff_decode(page_table, lam_p, subln_g, q, k_new, v_new, cache_k, cache_v, lam_init):
    bs, tq, w = q.shape
    n_pages_total = page_table.shape[1]
    pc = PAGES_PER_STEP
    rows = H_A * 2 * tq
    dv = 2 * DH_A
    page_rows = cache_k.shape[1]
    per_b = lambda b, j, pt: (b, 0, 0)
    const = lambda b, j, pt: (0, 0)
    grid_spec = pltpu.PrefetchScalarGridSpec(
        num_scalar_prefetch=1,
        grid=(bs, n_pages_total // pc),
        in_specs=[pl.BlockSpec(lam_p.shape, const), pl.BlockSpec((1, dv), const),
                  pl.BlockSpec((None, tq, w), per_b),
                  pl.BlockSpec((None,) + k_new.shape[1:], per_b),
                  pl.BlockSpec((None,) + v_new.shape[1:], per_b)]
        + _page_specs(page_rows, pc) + _page_specs(page_rows, pc),
        out_specs=pl.BlockSpec((None, tq, w), per_b),
        scratch_shapes=[pltpu.VMEM((rows, dv), BF16), pltpu.VMEM((rows, page_rows), F32),
                        pltpu.VMEM((rows, 1), F32), pltpu.VMEM((rows, 1), F32), pltpu.VMEM((rows, dv), F32)])
    return pl.pallas_call(
        functools.partial(_diff_decode_body, n_pages=pc, tq=tq, lam_init=lam_init),
        grid_spec=grid_spec,
        out_shape=jax.ShapeDtypeStruct((bs, tq, w), F32),
        compiler_params=_cparams(("parallel", "arbitrary"), 48),
        name="diff_decode",
    )(page_table, lam_p, subln_g, q, k_new, v_new, *([cache_k] * pc), *([cache_v] * pc))


def _chunk_perm(page_tokens):
    n = page_tokens * G_B
    chunks = page_tokens // CMP_STRIDE
    out = np.arange(n)
    l, g, c = out // (G_B * chunks), (out // chunks) % G_B, out % chunks
    src = (c * CMP_STRIDE + l) * G_B + g
    perm = np.zeros((n, n), np.float32)
    perm[out, src] = 1.0
    return jnp.asarray(perm, BF16)


def _compress_decode_body(pt_ref, perm_ref, wk_ref, wv_ref, c_ref, *rest, n_pages):
    k_pages = rest[:n_pages]
    v_pages = rest[n_pages:2 * n_pages]
    yk_ref, yv_ref, a_scr = rest[2 * n_pages:]
    chunks = k_pages[0].shape[0] // (G_B * CMP_STRIDE)
    per_g = n_pages * chunks
    for a, (pages, w_ref, y_ref) in enumerate(((k_pages, wk_ref, yk_ref), (v_pages, wv_ref, yv_ref))):
        side_by_side = jnp.concatenate([pages[p][...].astype(BF16) for p in range(n_pages)], axis=1)
        pp = _dot(perm_ref[...], side_by_side)
        for l in range(CMP_STRIDE):
            for g in range(G_B):
                src = (l * G_B + g) * chunks
                for p in range(n_pages):
                    dst = g * per_g + p * chunks
                    a_scr[dst:dst + chunks, l * DH_B:(l + 1) * DH_B] = pp[src:src + chunks, p * DH_B:(p + 1) * DH_B]
        y = _dot(a_scr[...].astype(BF16), w_ref[...])
        y = jnp.concatenate([y[:, :DH_B] + c_ref[a, 0:1, :], y[:, DH_B:]], axis=1)
        for g in range(G_B):
            y_ref[g] = y[g * per_g:(g + 1) * per_g]


def _compress_decode(page_table, perm, wk, wv, const, cache_k, cache_v):
    bs, n_pages_total = page_table.shape
    pc = GROUP_PAGES_PER_STEP
    page_rows = cache_k.shape[1]
    chunks = page_rows // (G_B * CMP_STRIDE)
    n_chunk = n_pages_total * chunks
    const2 = lambda b, j, pt: (0, 0)
    y_spec = pl.BlockSpec((None, G_B, pc * chunks, 2 * DH_B), lambda b, j, pt: (b, 0, j, 0))
    y_shape = jax.ShapeDtypeStruct((bs, G_B, n_chunk, 2 * DH_B), F32)
    grid_spec = pltpu.PrefetchScalarGridSpec(
        num_scalar_prefetch=1,
        grid=(bs, n_pages_total // pc),
        in_specs=[pl.BlockSpec(perm.shape, const2), pl.BlockSpec(wk.shape, const2), pl.BlockSpec(wv.shape, const2),
                  pl.BlockSpec(const.shape, lambda b, j, pt: (0, 0, 0))]
        + _page_specs(page_rows, pc) + _page_specs(page_rows, pc),
        out_specs=[y_spec, y_spec],
        scratch_shapes=[pltpu.VMEM((G_B * pc * chunks, CMP_STRIDE * DH_B), F32)])
    return pl.pallas_call(
        functools.partial(_compress_decode_body, n_pages=pc),
        grid_spec=grid_spec,
        out_shape=[y_shape, y_shape],
        compiler_params=_cparams(("parallel", "arbitrary"), 40),
        name="compress_decode",
    )(page_table, perm, wk, wv, const, *([cache_k] * pc), *([cache_v] * pc))


def _nsa_decode_body(pt_ref, q_ref, yk_ref, yv_ref, selmap_ref, expand_ref, gate_ref, ksn_ref, vsn_ref, kwn_ref,
                     vwn_ref, wk_ref, wv_ref, *rest, n_pages, tq, past_len, n_sel, n_top):
    k_pages = rest[:n_pages]
    v_pages = rest[n_pages:2 * n_pages]
    o_ref, wko_ref, wvo_ref, q_scr, sel_scr, bias_scr, oc_scr, m_scr, l_scr, acc_scr = rest[2 * n_pages:]
    j = pl.program_id(1)
    rows = H_B * tq
    rows_g = HPG_B * tq
    n_keys = k_pages[0].shape[0]
    log_tq = int(math.log2(tq))
    log_g = int(math.log2(G_B))

    def group_of_row(shape):
        return lax.shift_right_logical(lax.broadcasted_iota(jnp.int32, shape, 0), int(math.log2(rows_g)))

    def qpos_of_row(shape):
        return past_len + (lax.broadcasted_iota(jnp.int32, shape, 0) & (tq - 1))

    @pl.when(j == 0)
    def _():
        q = q_ref[...]
        q_scr[...] = jnp.concatenate([q[:, h * DH_B:(h + 1) * DH_B] for h in range(H_B)], axis=0).astype(BF16)
        qpos_col = past_len + lax.broadcasted_iota(jnp.int32, (tq, 1), 0)
        for g in range(G_B):
            qg = q_scr[g * rows_g:(g + 1) * rows_g, :]
            kcmp = _compressed_kv(yk_ref[g])
            vcmp = _compressed_kv(yv_ref[g])
            n_cmp = kcmp.shape[0]
            s = _dot_nt(qg, kcmp)
            cmp_end = lax.broadcasted_iota(jnp.int32, (rows_g, n_cmp), 1) * CMP_STRIDE + (CMP_BLOCK - 1)
            p = _masked_softmax(s, cmp_end <= qpos_of_row((rows_g, n_cmp)))
            oc_scr[g * rows_g:(g + 1) * rows_g, :] = _dot(p.astype(BF16), vcmp)
            p_sum = p[0:tq]
            for h in range(1, HPG_B):
                p_sum = p_sum + p[h * tq:(h + 1) * tq]
            imp = jnp.dot(p_sum, selmap_ref[...], precision=lax.Precision.HIGHEST, preferred_element_type=F32)
            sel = _select_blocks(imp, qpos_col, n_sel, n_top)
            for h in range(HPG_B):
                sel_scr[g * rows_g + h * tq:g * rows_g + (h + 1) * tq, :] = sel
        sel_rows = sel_scr[...]
        n_lanes = sel_rows.shape[1]
        exp_slots = expand_ref.shape[1]
        blocks_per_exp = exp_slots // (SEL_BLOCK * G_B)
        slot = lax.broadcasted_iota(jnp.int32, (rows, exp_slots), 1)
        same_group = group_of_row((rows, exp_slots)) == (slot & (G_B - 1))
        for c in range(bias_scr.shape[1] // exp_slots):
            shifted = pltpu.roll(sel_rows, (n_lanes - c * blocks_per_exp) % n_lanes, 1) if c else sel_rows
            chosen = _dot(shifted[:, :LANES].astype(BF16), expand_ref[...])
            bias_scr[:, c * exp_slots:(c + 1) * exp_slots] = jnp.where((chosen > 0.5) & same_group, 0.0, NEG)
        m_scr[...] = jnp.full(m_scr.shape, NEG, F32)
        l_scr[...] = jnp.zeros(l_scr.shape, F32)
        acc_scr[...] = jnp.zeros(acc_scr.shape, F32)

    qq = q_scr[...]
    step_slots = n_pages * n_keys
    bias = bias_scr[:, pl.ds(pl.multiple_of(j * step_slots, step_slots), step_slots)]
    ss = [_dot_nt(qq, k_pages[r][...].astype(BF16)) + bias[:, r * n_keys:(r + 1) * n_keys] for r in range(n_pages)]
    vs = [v_pages[r][...].astype(BF16) for r in range(n_pages)]
    m_scr[...], l_scr[...], acc_scr[...] = _online_update_pages(ss, vs, m_scr[...], l_scr[...], acc_scr[...])

    @pl.when(j == pl.num_programs(1) - 1)
    def _():
        sel_rows = sel_scr[...]
        sel_lane = lax.broadcasted_iota(jnp.int32, sel_rows.shape, 1)

        def block_selected(blk):
            return jnp.sum(jnp.where(sel_lane == blk, sel_rows, 0.0), axis=1, keepdims=True) > 0.5

        n_new = ksn_ref.shape[0]
        c_new = lax.broadcasted_iota(jnp.int32, (rows, n_new), 1)
        t_new = lax.shift_right_logical(c_new, log_g)
        grp_ok = group_of_row((rows, n_new)) == (c_new & (G_B - 1))
        causal = (past_len + t_new) <= qpos_of_row((rows, n_new))
        ok = grp_ok & causal & block_selected(past_len // SEL_BLOCK)
        s = jnp.where(ok, _dot_nt(qq, ksn_ref[...].astype(BF16)), NEG)
        _, l, acc = _online_update(s, vsn_ref[...].astype(BF16), m_scr[...], l_scr[...], acc_scr[...])
        o_s = acc / l

        n_win = wk_ref.shape[0]
        win_tokens = n_win // G_B
        c_w = lax.broadcasted_iota(jnp.int32, (rows, n_win), 1)
        kpos = past_len - win_tokens + lax.shift_right_logical(c_w, log_g)
        dist = qpos_of_row((rows, n_win)) - kpos
        ok_w = (group_of_row((rows, n_win)) == (c_w & (G_B - 1))) & (dist >= 0) & (dist <= WINDOW) & (kpos >= 0)
        s_w = jnp.where(ok_w, _dot_nt(qq, wk_ref[...].astype(BF16)), NEG)
        dist_n = qpos_of_row((rows, n_new)) - (past_len + t_new)
        ok_n = grp_ok & (dist_n >= 0) & (dist_n <= WINDOW)
        s_n = jnp.where(ok_n, _dot_nt(qq, kwn_ref[...].astype(BF16)), NEG)
        m = jnp.maximum(jnp.max(s_w, axis=-1, keepdims=True), jnp.max(s_n, axis=-1, keepdims=True))
        e_w = jnp.where(ok_w, jnp.exp2(s_w - m), 0.0)
        e_n = jnp.where(ok_n, jnp.exp2(s_n - m), 0.0)
        l_w = jnp.sum(e_w, axis=-1, keepdims=True) + jnp.sum(e_n, axis=-1, keepdims=True)
        o_w = (_dot(e_w.astype(BF16), wv_ref[...].astype(BF16))
               + _dot(e_n.astype(BF16), vwn_ref[...].astype(BF16))) / jnp.where(l_w > 0.0, l_w, 1.0)

        gates = gate_ref[...]
        o_c = oc_scr[...]
        for head in range(H_B):
            rs = slice(head * tq, (head + 1) * tq)
            o_ref[:, head * DH_B:(head + 1) * DH_B] = (
                gates[:, 3 * head:3 * head + 1] * o_c[rs] + gates[:, 3 * head + 1:3 * head + 2] * o_s[rs]
                + gates[:, 3 * head + 2:3 * head + 3] * o_w[rs])

        keep = n_win - n_new
        wko_ref[0:keep, :] = wk_ref[n_new:n_win, :]
        wko_ref[keep:n_win, :] = kwn_ref[...]
        wvo_ref[0:keep, :] = wv_ref[n_new:n_win, :]
        wvo_ref[keep:n_win, :] = vwn_ref[...]


def _nsa_decode(page_table, q, yk, yv, selmap, gates, ks_new, vs_new, kw_new, vw_new, win_k, win_v,
                cache_k, cache_v, past_len, n_sel):
    bs, tq, w = q.shape
    n_pages_total = page_table.shape[1]
    pc = GROUP_PAGES_PER_STEP
    rows = H_B * tq
    page_rows = cache_k.shape[1]
    exp_slots = 2048
    slots_per_block = SEL_BLOCK * G_B
    expand = jnp.asarray(np.arange(LANES)[:, None] == (np.arange(exp_slots)[None, :] // slots_per_block), BF16)
    assert (n_pages_total * page_rows) % exp_slots == 0 and exp_slots // slots_per_block <= LANES
    per_b3 = lambda b, j, pt: (b, 0, 0)
    per_b4 = lambda b, j, pt: (b, 0, 0, 0)
    const2 = lambda b, j, pt: (0, 0)
    blk3 = lambda a: pl.BlockSpec((None,) + a.shape[1:], per_b3)
    y_spec = pl.BlockSpec((None,) + yk.shape[1:], per_b4)
    grid_spec = pltpu.PrefetchScalarGridSpec(
        num_scalar_prefetch=1,
        grid=(bs, n_pages_total // pc),
        in_specs=[blk3(q), y_spec, y_spec, pl.BlockSpec(selmap.shape, const2), pl.BlockSpec(expand.shape, const2),
                  blk3(gates), blk3(ks_new), blk3(vs_new), blk3(kw_new), blk3(vw_new), blk3(win_k), blk3(win_v)]
        + _page_specs(page_rows, pc) + _page_specs(page_rows, pc),
        out_specs=[blk3(q), blk3(win_k), blk3(win_v)],
        scratch_shapes=[pltpu.VMEM((rows, DH_B), BF16), pltpu.VMEM((rows, selmap.shape[1]), F32),
                        pltpu.VMEM((rows, n_pages_total * page_rows), F32),
                        pltpu.VMEM((rows, DH_B), F32), pltpu.VMEM((rows, 1), F32), pltpu.VMEM((rows, 1), F32),
                        pltpu.VMEM((rows, DH_B), F32)])
    return pl.pallas_call(
        functools.partial(_nsa_decode_body, n_pages=pc, tq=tq, past_len=past_len, n_sel=n_sel,
                          n_top=min(SEL_TOP_N, n_sel)),
        grid_spec=grid_spec,
        out_shape=[jax.ShapeDtypeStruct((bs, tq, w), F32), jax.ShapeDtypeStruct(win_k.shape, F32),
                   jax.ShapeDtypeStruct(win_v.shape, F32)],
        compiler_params=_cparams(("parallel", "arbitrary"), 48),
        name="nsa_decode",
    )(page_table, q, yk, yv, selmap, expand, gates, ks_new, vs_new, kw_new, vw_new, win_k, win_v,
      *([cache_k] * pc), *([cache_v] * pc))


def kernel(x_prompt, x_sample, cache_diff_k, cache_diff_v, cache_cmp_k, cache_cmp_v, cache_sel_k, cache_sel_v,
           cache_win_k, cache_win_v, page_table, c_prompt, c_sample, w_mod, b_mod, norm_g, w_in, w_out, lam_p,
           subln_g, cmp_pe, cmp_phi, ffn_gate, ffn_up, ffn_down):
    depth = w_mod.shape[0]
    assert depth == 1, "single-layer step"
    B, T, D = x_prompt.shape
    Bs, Ts, _ = x_sample.shape
    n_pool, page = cache_diff_k.shape[1], cache_diff_k.shape[2]
    past_len = page_table.shape[1] * page
    win_len = cache_win_k.shape[2]
    assert T % 512 == 0 and win_len <= T and win_len == WINDOW and past_len >= win_len
    assert past_len % SEL_BLOCK == 0 and Ts <= SEL_BLOCK and (Ts & (Ts - 1)) == 0
    assert (past_len + Ts - CMP_BLOCK) // CMP_STRIDE + 1 <= past_len // CMP_STRIDE
    lam_init = 0.8 - 0.6 * math.exp(-0.3 * 0)
    Np, Ns = B * T, Bs * Ts
    gq = G_B * DH_B

    wg, wu, wd = ffn_gate[0].astype(BF16), ffn_up[0].astype(BF16), ffn_down[0].astype(BF16)
    w_pad = jnp.pad(w_in[0].astype(BF16), ((0, 0), (0, N_IN_BLOCKS * IN_BLOCK - w_in.shape[2])))
    w_out_a, w_out_b = w_out[0, :H_A * 2 * DH_A].astype(BF16), w_out[0, H_A * 2 * DH_A:].astype(BF16)
    ng = norm_g[0].reshape(6, 1, D)
    wck, pek = _compress_weights(cmp_pe[0, 0], cmp_phi[0, 0])
    wcv, pev = _compress_weights(cmp_pe[0, 1], cmp_phi[0, 1])
    cmp_const = _cmp_const(pek, pev, wck, wcv)

    n_c = B + Bs
    c_all = jnp.pad(jnp.concatenate([c_prompt, c_sample], axis=0), ((0, -n_c % 8), (0, 0)))
    mod = _adaln_mod(c_all, w_mod[0], b_mod[0][None]).reshape(c_all.shape[0], 9, D)
    mod_p = mod[:B].transpose(1, 0, 2).reshape(9, B, 1, D)
    mod_s = jnp.repeat(mod[B:n_c].transpose(1, 0, 2), Ts, axis=1)

    def split_states(pr, batch, seq):
        return (pr["k_a"].reshape(1, batch, seq, H_A, 2 * DH_A), pr["v_a"].reshape(1, batch, seq, H_A, 2 * DH_A),
                pr["kc"].reshape(1, batch, seq, G_B, DH_B), pr["vc"].reshape(1, batch, seq, G_B, DH_B),
                pr["ks"].reshape(1, batch, seq, G_B, DH_B), pr["vs"].reshape(1, batch, seq, G_B, DH_B))

    names = ("q_a", "k_a", "k_a_bf", "v_a", "v_a_bf", "q_b", "kc", "vc", "ks", "vs", "kw", "vw", "nkv", "gates")

    tm_p = 512
    xp = x_prompt.reshape(Np, D)
    x1 = _ffn(xp, mod_p, (0, 1, 2), T, ng[0], ng[1], wg, wu, wd, 0, tm_p, FFN_TF)
    pr = dict(zip(names, _inproj(x1, mod_p, 3, 4, T, ng[2], w_pad, _rope_tables(np.arange(T)), tm_p)))
    o_a = _diff_prompt(pr["q_a"], pr["k_a_bf"], pr["v_a_bf"], lam_p[0], subln_g[0][None], B, T, lam_init)
    yk, yv = _compress_prompt(pr["kc"], pr["vc"], wck, wcv, cmp_const, B, T)
    n_cmp_p = (T - CMP_BLOCK) // CMP_STRIDE + 1
    n_sel_p = -(-T // SEL_BLOCK)
    assert n_sel_p <= LANES
    selmap_p = _sel_map(T // CMP_STRIDE, n_cmp_p, n_sel_p, n_sel_p).T
    o_b = _nsa_prompt(pr["q_b"], yk, yv, pr["nkv"], pr["gates"], selmap_p, B, T)
    x2 = _outproj(o_a, o_b, w_out_a, w_out_b, x1, mod_p, 5, T, ng[3], tm_p)
    y_p = _ffn(x2, mod_p, (6, 7, 8), T, ng[4], ng[5], wg, wu, wd, 1, tm_p, FFN_TF).reshape(B, T, D)
    st_p = split_states(pr, B, T)
    wk_p = pr["kw"].reshape(1, B, T, G_B, DH_B)[:, :, T - win_len:]
    wv_p = pr["vw"].reshape(1, B, T, G_B, DH_B)[:, :, T - win_len:]

    xs = x_sample.reshape(Ns, D)
    x1s = _ffn(xs, mod_s, (0, 1, 2), 1, ng[0], ng[1], wg, wu, wd, 0, Ns, FFN_TF)
    pos_s = np.tile(past_len + np.arange(Ts), Bs)
    ps = dict(zip(names, _inproj(x1s, mod_s, 3, 4, 1, ng[2], w_pad, _rope_tables(pos_s), Ns)))
    ck = cache_diff_k[0].reshape(n_pool, page * H_A, 2 * DH_A)
    cv = cache_diff_v[0].reshape(n_pool, page * H_A, 2 * DH_A)
    o_as = _diff_decode(page_table, lam_p[0], subln_g[0][None],
                        ps["q_a"].astype(F32).reshape(Bs, Ts, H_A * 2 * DH_A),
                        ps["k_a"].reshape(Bs, Ts * H_A, 2 * DH_A), ps["v_a"].reshape(Bs, Ts * H_A, 2 * DH_A),
                        ck, cv, lam_init)
    pool3 = lambda c: c[0].reshape(n_pool, page * G_B, DH_B)
    yks, yvs = _compress_decode(page_table, _chunk_perm(page), wck, wcv, cmp_const,
                                pool3(cache_cmp_k), pool3(cache_cmp_v))
    tk_s = past_len + Ts
    n_cmp_s = (tk_s - CMP_BLOCK) // CMP_STRIDE + 1
    n_sel_s = -(-tk_s // SEL_BLOCK)
    selmap_s = _sel_map(past_len // CMP_STRIDE, n_cmp_s, n_sel_s, -(-n_sel_s // LANES) * LANES)
    new3 = lambda a: a.reshape(Bs, Ts * G_B, DH_B)
    o_bs, wk_s, wv_s = _nsa_decode(
        page_table, ps["q_b"].astype(F32).reshape(Bs, Ts, H_B * DH_B), yks, yvs, selmap_s,
        ps["gates"].reshape(Bs, Ts, LANES), new3(ps["ks"]), new3(ps["vs"]), new3(ps["kw"]), new3(ps["vw"]),
        cache_win_k[0].reshape(Bs, win_len * G_B, DH_B), cache_win_v[0].reshape(Bs, win_len * G_B, DH_B),
        pool3(cache_sel_k), pool3(cache_sel_v), past_len, n_sel_s)
    x2s = _outproj(o_as.reshape(Ns, -1).astype(BF16), o_bs.reshape(Ns, -1).astype(BF16), w_out_a, w_out_b,
                   x1s, mod_s, 5, 1, ng[3], Ns)
    y_s = _ffn(x2s, mod_s, (6, 7, 8), 1, ng[4], ng[5], wg, wu, wd, 1, Ns, FFN_TF).reshape(Bs, Ts, D)
    st_s = split_states(ps, Bs, Ts)
    wk_s = wk_s.reshape(1, Bs, win_len, G_B, DH_B)
    wv_s = wv_s.reshape(1, Bs, win_len, G_B, DH_B)

    return (y_p, y_s) + st_p + (wk_p, wv_p) + st_s + (wk_s, wv_s)
```

```python
import functools
import math

import numpy as np
import jax
import jax.numpy as jnp
from jax import lax
from jax.experimental import pallas as pl
from jax.experimental.pallas import tpu as pltpu

F32 = jnp.float32
BF16 = jnp.bfloat16

DH_A = 64
H_A = 8
DH_B = 128
H_B = 8
G_B = 2
HPG_B = H_B // G_B
CMP_BLOCK = 32
CMP_STRIDE = 16
SEL_BLOCK = 64
SEL_TOP_N = 16
N_LOCAL = 2
WINDOW = 512
ROPE_THETA = 10000.0
EPS = 1e-6
NEG = -1e30
FORCE = 1e4
LOG2E = math.log2(math.e)

LANES = 128
MIB = 1024 * 1024
PAGES_PER_STEP = 16
GROUP_PAGES_PER_STEP = 16
FFN_TF = 1024


def _dot(a, b):
    return jnp.dot(a, b, preferred_element_type=F32)


def _dot_nt(a, b):
    return lax.dot_general(a, b, (((1,), (1,)), ((), ())), preferred_element_type=F32)


def _rms(x, g):
    return x * lax.rsqrt(jnp.mean(x * x, axis=-1, keepdims=True) + EPS) * g


def _cparams(sem, vmem_mib):
    return pltpu.CompilerParams(dimension_semantics=sem, vmem_limit_bytes=vmem_mib * MIB)


def _mod_body(c_ref, w_ref, b_ref, o_ref):
    c = c_ref[...]
    a = (c * jax.nn.sigmoid(c)).astype(BF16)
    o_ref[...] = _dot(a, w_ref[...].astype(BF16)) + b_ref[...]


def _adaln_mod(c, w_mod, b_mod):
    R, D = c.shape
    n_out = w_mod.shape[1]
    tn = 1024
    return pl.pallas_call(
        _mod_body,
        grid=(n_out // tn,),
        in_specs=[pl.BlockSpec((R, D), lambda j: (0, 0)),
                  pl.BlockSpec((D, tn), lambda j: (0, j)),
                  pl.BlockSpec((1, tn), lambda j: (0, j))],
        out_specs=pl.BlockSpec((R, tn), lambda j: (0, j)),
        out_shape=jax.ShapeDtypeStruct((R, n_out), F32),
        compiler_params=_cparams(("arbitrary",), 40),
        name="adaln_mod",
    )(c, w_mod, b_mod)


def _mod_spec(mod, k, tm, rows_per_group):
    if mod.ndim == 4:
        return pl.BlockSpec((None, None, 1, mod.shape[-1]),
                            lambda i, *_, k=k: (k, (i * tm) // rows_per_group, 0, 0))
    return pl.BlockSpec((None, tm, mod.shape[-1]), lambda i, *_, k=k: (k, i, 0))


def _ffn_body(x_ref, sh_ref, sc_ref, gt_ref, gpre_ref, gpost_ref, wg_ref, wu_ref, wd_ref, o_ref,
              h_scr, acc_scr, *, n_chunk):
    f = pl.program_id(1)

    @pl.when(f == 0)
    def _():
        h = _rms(x_ref[...], gpre_ref[...]) * (1.0 + sc_ref[...]) + sh_ref[...]
        h_scr[...] = h.astype(BF16)
        acc_scr[...] = jnp.zeros(acc_scr.shape, F32)

    h = h_scr[...]
    tf = wg_ref.shape[1]
    hw = min(tf, 512)
    parts = []
    for c in range(tf // hw):
        g = _dot(h, wg_ref[:, c * hw:(c + 1) * hw])
        u = _dot(h, wu_ref[:, c * hw:(c + 1) * hw])
        parts.append((g * jax.nn.sigmoid(g) * u).astype(BF16))
    a = parts[0] if len(parts) == 1 else jnp.concatenate(parts, axis=1)
    d = acc_scr.shape[1]
    cw = d // n_chunk
    for n in range(n_chunk):
        acc_scr[:, n * cw:(n + 1) * cw] += _dot(a, wd_ref[:, n * cw:(n + 1) * cw])

    @pl.when(f == pl.num_programs(1) - 1)
    def _():
        o_ref[...] = x_ref[...] + 0.5 * gt_ref[...] * _rms(acc_scr[...], gpost_ref[...])


def _ffn(x, mod, ks, rows_per_group, g_pre, g_post, wg, wu, wd, layer, tm, tf):
    n, d = x.shape
    dff = wg.shape[2]
    row = lambda i, f: (i, 0)
    const = lambda i, f: (0, 0)
    return pl.pallas_call(
        functools.partial(_ffn_body, n_chunk=4),
        grid=(n // tm, dff // tf),
        in_specs=[pl.BlockSpec((tm, d), row),
                  _mod_spec(mod, ks[0], tm, rows_per_group),
                  _mod_spec(mod, ks[1], tm, rows_per_group),
                  _mod_spec(mod, ks[2], tm, rows_per_group),
                  pl.BlockSpec((1, d), const), pl.BlockSpec((1, d), const),
                  pl.BlockSpec((None, d, tf), lambda i, f: (layer, 0, f)),
                  pl.BlockSpec((None, d, tf), lambda i, f: (layer, 0, f)),
                  pl.BlockSpec((None, tf, d), lambda i, f: (layer, f, 0))],
        out_specs=pl.BlockSpec((tm, d), row),
        out_shape=jax.ShapeDtypeStruct((n, d), F32),
        scratch_shapes=[pltpu.VMEM((tm, d), BF16), pltpu.VMEM((tm, d), F32)],
        compiler_params=_cparams(("parallel", "arbitrary"), 60),
        name="ffn",
    )(x, mod, mod, mod, g_pre, g_post, wg, wu, wd)


IN_BLOCK = 1024
N_IN_BLOCKS = 6


def _inproj_body(x_ref, sh_ref, sc_ref, g_ref, w_ref, ca_ref, sa_ref, cb_ref, sb_ref,
                 qa_ref, ka_ref, kab_ref, va_ref, vab_ref, qb_ref, kc_ref, vc_ref, ks_ref, vs_ref,
                 kw_ref, vw_ref, nkv_ref, gate_ref, h_scr):
    j = pl.program_id(1)

    @pl.when(j == 0)
    def _():
        h = _rms(x_ref[...], g_ref[...]) * (1.0 + sc_ref[...]) + sh_ref[...]
        h_scr[...] = h.astype(BF16)

    lane = lax.broadcasted_iota(jnp.int32, (1, LANES), 1)
    low32 = (lane & (DH_A - 1)) < (DH_A // 2)

    def cols(c):
        y = _dot(h_scr[...], w_ref[:, c * 256:(c + 1) * 256])
        return y[:, :LANES], y[:, LANES:]

    def rope64(y):
        partner = jnp.where(low32, pltpu.roll(y, LANES - DH_A // 2, 1), pltpu.roll(y, DH_A // 2, 1))
        return y * ca_ref[...] + partner * sa_ref[...]

    def rope128(y):
        return y * cb_ref[...] + pltpu.roll(y, DH_B // 2, 1) * sb_ref[...]

    def tile(t):
        return slice(t * LANES, (t + 1) * LANES)

    @pl.when(j == 0)
    def _():
        for c in range(4):
            for t, y in enumerate(cols(c)):
                qa_ref[:, tile(2 * c + t)] = (rope64(y) * (DH_A ** -0.5 * LOG2E)).astype(BF16)

    @pl.when(j == 1)
    def _():
        for c in range(4):
            for t, y in enumerate(cols(c)):
                r = rope64(y)
                ka_ref[:, tile(2 * c + t)] = r
                kab_ref[:, tile(2 * c + t)] = r.astype(BF16)

    @pl.when(j == 2)
    def _():
        for c in range(4):
            for t, y in enumerate(cols(c)):
                va_ref[:, tile(2 * c + t)] = y
                vab_ref[:, tile(2 * c + t)] = y.astype(BF16)

    @pl.when(j == 3)
    def _():
        for c in range(4):
            for t, y in enumerate(cols(c)):
                qb_ref[:, tile(2 * c + t)] = (rope128(y) * (DH_B ** -0.5 * LOG2E)).astype(BF16)

    tm = x_ref.shape[0]

    def group_rows(g):
        return pl.ds(g, tm, stride=G_B)

    @pl.when(j == 4)
    def _():
        for c, (ref, roped) in enumerate(((kc_ref, True), (vc_ref, False), (ks_ref, True), (vs_ref, False))):
            for t, y in enumerate(cols(c)):
                r = rope128(y) if roped else y
                ref[group_rows(t), :] = r
                if c >= 2:
                    nkv_ref[:, tile(2 * (c - 2) + t)] = r.astype(BF16)

    @pl.when(j == 5)
    def _():
        for c, (ref, roped) in enumerate(((kw_ref, True), (vw_ref, False))):
            for t, y in enumerate(cols(c)):
                r = rope128(y) if roped else y
                ref[group_rows(t), :] = r
                nkv_ref[:, tile(4 + 2 * c + t)] = r.astype(BF16)
        gate_ref[...] = jax.nn.sigmoid(cols(2)[0])


def _inproj(x, mod, k_shift, k_scale, rows_per_group, g, w_pad, tabs, tm):
    n, d = x.shape
    gq = G_B * DH_B
    n_tab = tabs[0].shape[0] // tm
    row = lambda i, j: (i, 0)
    const = lambda i, j: (0, 0)
    tab = pl.BlockSpec((tm, LANES), lambda i, j: (i % n_tab, 0))
    wide = lambda dt: jax.ShapeDtypeStruct((n, IN_BLOCK), dt)
    grouped = jax.ShapeDtypeStruct((n * G_B, DH_B), F32)
    out_shape = [wide(BF16), wide(F32), wide(BF16), wide(F32), wide(BF16), wide(BF16)] + [grouped] * 6 + [
        wide(BF16), jax.ShapeDtypeStruct((n, LANES), F32)]
    out_specs = [pl.BlockSpec((tm * s.shape[0] // n, s.shape[1]), row) for s in out_shape]
    return pl.pallas_call(
        _inproj_body,
        grid=(n // tm, N_IN_BLOCKS),
        in_specs=[pl.BlockSpec((tm, d), row),
                  _mod_spec(mod, k_shift, tm, rows_per_group),
                  _mod_spec(mod, k_scale, tm, rows_per_group),
                  pl.BlockSpec((1, d), const),
                  pl.BlockSpec((d, IN_BLOCK), lambda i, j: (0, j)),
                  tab, tab, tab, tab],
        out_specs=out_specs,
        out_shape=out_shape,
        scratch_shapes=[pltpu.VMEM((tm, d), BF16)],
        compiler_params=_cparams(("parallel", "arbitrary"), 56),
        name="inproj",
    )(x, mod, mod, g, w_pad, *tabs)


def _rope_tables(pos):
    pos = np.asarray(pos, np.float64)[:, None]

    def tables(head_dim):
        half = head_dim // 2
        inv = ROPE_THETA ** (-np.arange(half, dtype=np.float64) / half)
        ang = pos * inv[None, :]
        cos = np.concatenate([np.cos(ang), np.cos(ang)], axis=1)
        sin = np.concatenate([-np.sin(ang), np.sin(ang)], axis=1)
        rep = LANES // head_dim
        return (jnp.asarray(np.tile(cos, (1, rep)), F32), jnp.asarray(np.tile(sin, (1, rep)), F32))

    ca, sa = tables(DH_A)
    cb, sb = tables(DH_B)
    return ca, sa, cb, sb


def _lambda(lamp_ref, lam_init):
    lp = lamp_ref[...]
    a = jnp.sum(lp[0:1] * lp[1:2], axis=1, keepdims=True)
    b = jnp.sum(lp[2:3] * lp[3:4], axis=1, keepdims=True)
    return jnp.exp(a) - jnp.exp(b) + lam_init


def _online_update(s, v, m, l, acc):
    m_new = jnp.maximum(m, jnp.max(s, axis=-1, keepdims=True))
    p = jnp.exp2(s - m_new)
    alpha = jnp.exp2(m - m_new)
    l = alpha * l + jnp.sum(p, axis=-1, keepdims=True)
    acc = alpha * acc + _dot(p.astype(BF16), v)
    return m_new, l, acc


def _online_update_pages(ss, vs, m, l, acc):
    m_blk = jnp.max(ss[0], axis=-1, keepdims=True)
    for s in ss[1:]:
        m_blk = jnp.maximum(m_blk, jnp.max(s, axis=-1, keepdims=True))
    m_new = jnp.maximum(m, m_blk)
    alpha = jnp.exp2(m - m_new)
    l = alpha * l
    acc = alpha * acc
    for s, v in zip(ss, vs):
        p = jnp.exp2(s - m_new)
        l = l + jnp.sum(p, axis=-1, keepdims=True)
        acc = acc + _dot(p.astype(BF16), v)
    return m_new, l, acc


def _softmax_init(rows, dv):
    return (jnp.full((rows, 1), NEG, F32), jnp.zeros((rows, 1), F32), jnp.zeros((rows, dv), F32))


def _masked_softmax(s, mask):
    sm = jnp.where(mask, s, NEG)
    m = jnp.max(sm, axis=-1, keepdims=True)
    e = jnp.where(mask, jnp.exp2(sm - m), 0.0)
    l = jnp.sum(e, axis=-1, keepdims=True)
    return e / jnp.where(l > 0.0, l, 1.0)


def _select_blocks(imp, qpos, n_blocks, n_top):
    shape = imp.shape
    blk = lax.broadcasted_iota(jnp.int32, shape, 1)
    rel = lax.shift_right_logical(qpos, int(math.log2(SEL_BLOCK))) - blk
    visible = rel >= 0
    forced = (blk == 0) | (visible & (rel < N_LOCAL))
    score = jnp.where(visible, jnp.where(forced, FORCE + imp, imp), NEG)
    rank = jnp.zeros(shape, F32)
    for i in range(n_blocks):
        col = score[:, i:i + 1]
        beats = (col > score) | ((col == score) & (blk > i))
        rank = rank + jnp.where(beats, 1.0, 0.0)
    return jnp.where((rank < n_top) & visible, 1.0, 0.0)


def _select_blocks_t(imp_t, qpos_row, n_blocks, n_top):
    shape = imp_t.shape
    blk = lax.broadcasted_iota(jnp.int32, shape, 0)
    rel = lax.shift_right_logical(qpos_row, int(math.log2(SEL_BLOCK))) - blk
    visible = rel >= 0
    forced = (blk == 0) | (visible & (rel < N_LOCAL))
    score = jnp.where(visible, jnp.where(forced, FORCE + imp_t, imp_t), NEG)
    rank = jnp.zeros(shape, F32)
    for i in range(n_blocks):
        row = score[i:i + 1, :]
        beats = (row > score) | ((row == score) & (blk > i))
        rank = rank + jnp.where(beats, 1.0, 0.0)
    return jnp.where((rank < n_top) & visible, 1.0, 0.0)


def _compressed_kv(y):
    n = y.shape[0]
    return (y[:, :DH_B] + pltpu.roll(y[:, DH_B:], n - 1, 0)).astype(BF16)


def _diff_prompt_body(lamp_ref, g_ref, q_ref, k_ref, v_ref, o_ref, *, tq, lam_init):
    qi = pl.program_id(2)
    lam = _lambda(lamp_ref, lam_init)
    q = q_ref[...]
    lane = lax.broadcasted_iota(jnp.int32, (1, 2 * DH_A), 1)
    zero = jnp.zeros_like(q)
    qq = jnp.concatenate([jnp.where(lane < DH_A, q, zero), jnp.where(lane >= DH_A, q, zero)], axis=0)

    def chunk(j):
        start = pl.multiple_of(j * tq, tq)
        return k_ref[pl.ds(start, tq), :], v_ref[pl.ds(start, tq), :]

    def body(j, carry):
        k, v = chunk(j)
        return _online_update(_dot_nt(qq, k), v, *carry)

    carry = lax.fori_loop(0, qi, body, _softmax_init(2 * tq, 2 * DH_A))
    k, v = chunk(qi)
    s = _dot_nt(qq, k)
    r = lax.broadcasted_iota(jnp.int32, s.shape, 0) & (tq - 1)
    c = lax.broadcasted_iota(jnp.int32, s.shape, 1)
    m, l, acc = _online_update(jnp.where(c <= r, s, NEG), v, *carry)
    o = acc / l
    o_ref[...] = (_rms(o[:tq] - lam * o[tq:], g_ref[...]) * (1.0 - lam_init)).astype(o_ref.dtype)


def _diff_prompt(q, k, v, lam_p, subln_g, batch, seq, lam_init, tq=512):
    n, w = q.shape
    dv = 2 * DH_A
    nq = seq // tq
    return pl.pallas_call(
        functools.partial(_diff_prompt_body, tq=tq, lam_init=lam_init),
        grid=(batch, H_A, nq),
        in_specs=[pl.BlockSpec(lam_p.shape, lambda b, h, i: (0, 0)),
                  pl.BlockSpec((1, dv), lambda b, h, i: (0, 0)),
                  pl.BlockSpec((tq, dv), lambda b, h, i: (b * nq + i, h)),
                  pl.BlockSpec((seq, dv), lambda b, h, i: (b, h)),
                  pl.BlockSpec((seq, dv), lambda b, h, i: (b, h))],
        out_specs=pl.BlockSpec((tq, dv), lambda b, h, i: (b * nq + i, h)),
        out_shape=jax.ShapeDtypeStruct((n, w), BF16),
        compiler_params=_cparams(("parallel", "parallel", "arbitrary"), 48),
        name="diff_prompt",
    )(lam_p, subln_g, q, k, v)


def _cmp_const_body(pek_ref, pev_ref, wk_ref, wv_ref, o_ref):
    for a, (pe_ref, w_ref) in enumerate(((pek_ref, wk_ref), (pev_ref, wv_ref))):
        cw = _dot(pe_ref[...].astype(BF16), w_ref[...])
        o_ref[a] = jnp.broadcast_to(cw[0:1, :DH_B] + cw[1:2, DH_B:], (8, DH_B))


def _cmp_const(pek, pev, wk, wv):
    return pl.pallas_call(
        _cmp_const_body,
        out_shape=jax.ShapeDtypeStruct((2, 8, DH_B), F32),
        name="cmp_const",
    )(pek, pev, wk, wv)


def _compress_weights(pe, phi):
    half = CMP_BLOCK // 2
    w = jnp.concatenate([phi[:half].reshape(half * DH_B, DH_B), phi[half:].reshape(half * DH_B, DH_B)], axis=1)
    pe2 = jnp.zeros((8, half * DH_B), F32)
    pe2 = pe2.at[0].set(pe[:half].reshape(-1)).at[1].set(pe[half:].reshape(-1))
    return w.astype(BF16), pe2


def _compress_prompt_body(rk_ref, rv_ref, wk_ref, wv_ref, c_ref, yk_ref, yv_ref):
    half = CMP_BLOCK // 2
    for a, (r_ref, w_ref, y_ref) in enumerate(((rk_ref, wk_ref, yk_ref), (rv_ref, wv_ref, yv_ref))):
        parts = []
        for g in range(G_B):
            parts.append(jnp.concatenate(
                [r_ref[:, (l * G_B + g) * DH_B:(l * G_B + g + 1) * DH_B] for l in range(half)], axis=1))
        a_mat = jnp.concatenate(parts, axis=0).astype(BF16)
        y = _dot(a_mat, w_ref[...])
        y = jnp.concatenate([y[:, :DH_B] + c_ref[a, 0:1, :], y[:, DH_B:]], axis=1)
        n_chunk = y.shape[0] // G_B
        for g in range(G_B):
            y_ref[g] = y[g * n_chunk:(g + 1) * n_chunk]


def _compress_prompt(kc, vc, wk, wv, const, batch, seq):
    n_chunk = seq // CMP_STRIDE
    width = CMP_STRIDE * G_B * DH_B
    rk = kc.reshape(batch, n_chunk, width)
    rv = vc.reshape(batch, n_chunk, width)
    r_spec = pl.BlockSpec((None, n_chunk, width), lambda b: (b, 0, 0))
    w_spec = pl.BlockSpec(wk.shape, lambda b: (0, 0))
    y_spec = pl.BlockSpec((None, G_B, n_chunk, 2 * DH_B), lambda b: (b, 0, 0, 0))
    y_shape = jax.ShapeDtypeStruct((batch, G_B, n_chunk, 2 * DH_B), F32)
    return pl.pallas_call(
        _compress_prompt_body,
        grid=(batch,),
        in_specs=[r_spec, r_spec, w_spec, w_spec, pl.BlockSpec(const.shape, lambda b: (0, 0, 0))],
        out_specs=[y_spec, y_spec],
        out_shape=[y_shape, y_shape],
        compiler_params=_cparams(("parallel",), 40),
        name="compress_prompt",
    )(rk, rv, wk, wv, const)


def _nsa_prompt_body(q_ref, yk_ref, yv_ref, kv_ref, gate_ref, selmap_ref, o_ref, *, tq, n_sel, n_top):
    qi = pl.program_id(1)
    rows = HPG_B * tq
    gq = G_B * DH_B
    q0 = qi * tq
    qpos_col = q0 + lax.broadcasted_iota(jnp.int32, (tq, 1), 0)
    qpos_row = q0 + lax.broadcasted_iota(jnp.int32, (1, tq), 1)

    def stack_rows(x):
        return jnp.concatenate([x] * HPG_B, axis=0)

    gates = gate_ref[...]
    for g in range(G_B):
        qg = jnp.concatenate([q_ref[:, (g * HPG_B + h) * DH_B:(g * HPG_B + h + 1) * DH_B]
                              for h in range(HPG_B)], axis=0)

        kcmp = _compressed_kv(yk_ref[g])
        vcmp = _compressed_kv(yv_ref[g])
        n_cmp = kcmp.shape[0]
        s = _dot_nt(qg, kcmp)
        cmp_end = lax.broadcasted_iota(jnp.int32, (tq, n_cmp), 1) * CMP_STRIDE + (CMP_BLOCK - 1)
        p = _masked_softmax(s, stack_rows(jnp.where(cmp_end <= qpos_col, 1.0, 0.0)) > 0.5)
        o_c = _dot(p.astype(BF16), vcmp)
        p_sum = p[0:tq]
        for h in range(1, HPG_B):
            p_sum = p_sum + p[h * tq:(h + 1) * tq]
        imp_t = lax.dot_general(selmap_ref[...], p_sum, (((1,), (1,)), ((), ())),
                                precision=lax.Precision.HIGHEST, preferred_element_type=F32)
        sel_t = _select_blocks_t(imp_t, qpos_row, n_sel, n_top)
        sel = jnp.concatenate([sel_t, jnp.zeros((LANES - n_sel, tq), F32)], axis=0).T.astype(BF16)

        tk = 2 * tq
        blk_of_key = lax.shift_right_logical(lax.broadcasted_iota(jnp.int32, (LANES, tk), 1),
                                             int(math.log2(SEL_BLOCK)))
        blk_row = lax.broadcasted_iota(jnp.int32, (LANES, tk), 0)
        key_col_s = lax.broadcasted_iota(jnp.int32, (tq, tk), 1)

        def add_bias(s, bias):
            return (s.reshape(HPG_B, tq, s.shape[1]) + bias[None]).reshape(s.shape)

        def sel_step(j, carry):
            start = pl.multiple_of(j * tk, tk)
            k = kv_ref[pl.ds(start, tk), g * DH_B:(g + 1) * DH_B]
            v = kv_ref[pl.ds(start, tk), gq + g * DH_B:gq + (g + 1) * DH_B]
            expand = jnp.where(blk_row == blk_of_key + j * (tk // SEL_BLOCK), 1.0, 0.0).astype(BF16)
            ok = (_dot(sel, expand) > 0.5) & (key_col_s + j * tk <= qpos_col)
            return _online_update(add_bias(_dot_nt(qg, k), jnp.where(ok, 0.0, NEG)), v, *carry)

        n_sel_steps = lax.shift_right_logical(qi + 2, 1)
        _, l, acc = lax.fori_loop(0, n_sel_steps, sel_step, _softmax_init(rows, DH_B))
        o_s = acc / l

        span = tq + WINDOW
        w_start = pl.multiple_of(jnp.maximum(q0 - WINDOW, 0), tq)
        k = kv_ref[pl.ds(w_start, span), 2 * gq + g * DH_B:2 * gq + (g + 1) * DH_B]
        v = kv_ref[pl.ds(w_start, span), 3 * gq + g * DH_B:3 * gq + (g + 1) * DH_B]
        dist = qpos_col - (w_start + lax.broadcasted_iota(jnp.int32, (tq, span), 1))
        s = add_bias(_dot_nt(qg, k), jnp.where((dist >= 0) & (dist <= WINDOW), 0.0, NEG))
        p = jnp.exp2(s - jnp.max(s, axis=-1, keepdims=True))
        o_w = _dot(p.astype(BF16), v) / jnp.sum(p, axis=-1, keepdims=True)

        for h in range(HPG_B):
            head = g * HPG_B + h
            rs = slice(h * tq, (h + 1) * tq)
            o = (gates[:, 3 * head:3 * head + 1] * o_c[rs] + gates[:, 3 * head + 1:3 * head + 2] * o_s[rs]
                 + gates[:, 3 * head + 2:3 * head + 3] * o_w[rs])
            o_ref[:, head * DH_B:(head + 1) * DH_B] = o.astype(o_ref.dtype)


def _nsa_prompt(q, yk, yv, nkv, gates, selmap, batch, seq, tq=256):
    n, w = q.shape
    nq = seq // tq
    assert seq % (2 * tq) == 0 and WINDOW % tq == 0 and seq >= tq + WINDOW
    n_sel = -(-seq // SEL_BLOCK)
    n_chunk = yk.shape[2]
    y_spec = pl.BlockSpec((None, G_B, n_chunk, 2 * DH_B), lambda b, i: (b, 0, 0, 0))
    return pl.pallas_call(
        functools.partial(_nsa_prompt_body, tq=tq, n_sel=n_sel, n_top=min(SEL_TOP_N, n_sel)),
        grid=(batch, nq),
        in_specs=[pl.BlockSpec((tq, w), lambda b, i: (b * nq + i, 0)),
                  y_spec, y_spec,
                  pl.BlockSpec((seq, nkv.shape[1]), lambda b, i: (b, 0)),
                  pl.BlockSpec((tq, LANES), lambda b, i: (b * nq + i, 0)),
                  pl.BlockSpec(selmap.shape, lambda b, i: (0, 0))],
        out_specs=pl.BlockSpec((tq, w), lambda b, i: (b * nq + i, 0)),
        out_shape=jax.ShapeDtypeStruct((n, w), BF16),
        compiler_params=_cparams(("parallel", "arbitrary"), 48),
        name="nsa_prompt",
    )(q, yk, yv, nkv, gates, selmap)


def _sel_map(n_chunk, n_cmp, n_sel, n_sel_pad):
    i = np.arange(n_chunk)[:, None] * CMP_STRIDE
    j = np.arange(n_sel_pad)[None, :] * SEL_BLOCK
    m = (i < j + SEL_BLOCK) & (i + CMP_BLOCK > j)
    m &= (np.arange(n_chunk)[:, None] < n_cmp) & (np.arange(n_sel_pad)[None, :] < n_sel)
    return jnp.asarray(m.astype(np.float32))


def _outproj_body(oa_ref, ob_ref, wa_ref, wb_ref, x_ref, gt_ref, g_ref, o_ref):
    m = _dot(oa_ref[...], wa_ref[...]) + _dot(ob_ref[...], wb_ref[...])
    o_ref[...] = x_ref[...] + gt_ref[...] * _rms(m, g_ref[...])


def _outproj(o_a, o_b, w_a, w_b, x, mod, k_gate, rows_per_group, g, tm):
    n, d = x.shape
    row = lambda i: (i, 0)
    const = lambda i: (0, 0)
    return pl.pallas_call(
        _outproj_body,
        grid=(n // tm,),
        in_specs=[pl.BlockSpec((tm, o_a.shape[1]), row), pl.BlockSpec((tm, o_b.shape[1]), row),
                  pl.BlockSpec(w_a.shape, const), pl.BlockSpec(w_b.shape, const),
                  pl.BlockSpec((tm, d), row),
                  _mod_spec(mod, k_gate, tm, rows_per_group),
                  pl.BlockSpec((1, d), const)],
        out_specs=pl.BlockSpec((tm, d), row),
        out_shape=jax.ShapeDtypeStruct((n, d), F32),
        compiler_params=_cparams(("parallel",), 48),
        name="outproj",
    )(o_a, o_b, w_a, w_b, x, mod, g)


def _page_specs(rows, n_pages):
    return [pl.BlockSpec((None, rows, LANES), lambda b, j, pt, r=r: (pt[b, j * n_pages + r], 0, 0))
            for r in range(n_pages)]


def _diff_decode_body(pt_ref, lamp_ref, g_ref, q_ref, knew_ref, vnew_ref, *rest, n_pages, tq, lam_init):
    k_pages = rest[:n_pages]
    v_pages = rest[n_pages:2 * n_pages]
    o_ref, q_scr, bias_scr, m_scr, l_scr, acc_scr = rest[2 * n_pages:]
    j = pl.program_id(1)
    rows = H_A * 2 * tq
    n_keys = k_pages[0].shape[0]

    @pl.when(j == 0)
    def _():
        q = q_ref[...]
        lane = lax.broadcasted_iota(jnp.int32, (1, 2 * DH_A), 1)
        parts = []
        for h in range(H_A):
            qh = q[:, h * 2 * DH_A:(h + 1) * 2 * DH_A]
            parts += [jnp.where(lane < DH_A, qh, 0.0), jnp.where(lane >= DH_A, qh, 0.0)]
        q_scr[...] = jnp.concatenate(parts, axis=0).astype(BF16)
        head_of_row = lax.shift_right_logical(lax.broadcasted_iota(jnp.int32, (rows, n_keys), 0),
                                              int(math.log2(2 * tq)))
        head_of_key = lax.broadcasted_iota(jnp.int32, (rows, n_keys), 1) & (H_A - 1)
        bias_scr[...] = jnp.where(head_of_row == head_of_key, 0.0, NEG)
        m_scr[...] = jnp.full(m_scr.shape, NEG, F32)
        l_scr[...] = jnp.zeros(l_scr.shape, F32)
        acc_scr[...] = jnp.zeros(acc_scr.shape, F32)

    qq = q_scr[...]
    ss = [_dot_nt(qq, k_pages[r][...].astype(BF16)) + bias_scr[...] for r in range(n_pages)]
    vs = [v_pages[r][...].astype(BF16) for r in range(n_pages)]
    m_scr[...], l_scr[...], acc_scr[...] = _online_update_pages(ss, vs, m_scr[...], l_scr[...], acc_scr[...])

    @pl.when(j == pl.num_programs(1) - 1)
    def _():
        n_new = knew_ref.shape[0]
        r_i = lax.broadcasted_iota(jnp.int32, (rows, n_new), 0)
        c_i = lax.broadcasted_iota(jnp.int32, (rows, n_new), 1)
        ok = ((lax.shift_right_logical(r_i, int(math.log2(2 * tq))) == (c_i & (H_A - 1)))
              & (lax.shift_right_logical(c_i, int(math.log2(H_A))) <= (r_i & (tq - 1))))
        s = jnp.where(ok, _dot_nt(qq, knew_ref[...].astype(BF16)), NEG)
        _, l, acc = _online_update(s, vnew_ref[...].astype(BF16), m_scr[...], l_scr[...], acc_scr[...])
        o = acc / l
        lam = _lambda(lamp_ref, lam_init)
        for h in range(H_A):
            base = h * 2 * tq
            d = o[base:base + tq] - lam * o[base + tq:base + 2 * tq]
            o_ref[:, h * 2 * DH_A:(h + 1) * 2 * DH_A] = _rms(d, g_ref[...]) * (1.0 - lam_init)


def _diff_decode(page_table, lam_p, subln_g, q, k_new, v_new, cache_k, cache_v, lam_init):
    bs, tq, w = q.shape
    n_pages_total = page_table.shape[1]
    pc = PAGES_PER_STEP
    rows = H_A * 2 * tq
    dv = 2 * DH_A
    page_rows = cache_k.shape[1]
    per_b = lambda b, j, pt: (b, 0, 0)
    const = lambda b, j, pt: (0, 0)
    grid_spec = pltpu.PrefetchScalarGridSpec(
        num_scalar_prefetch=1,
        grid=(bs, n_pages_total // pc),
        in_specs=[pl.BlockSpec(lam_p.shape, const), pl.BlockSpec((1, dv), const),
                  pl.BlockSpec((None, tq, w), per_b),
                  pl.BlockSpec((None,) + k_new.shape[1:], per_b),
                  pl.BlockSpec((None,) + v_new.shape[1:], per_b)]
        + _page_specs(page_rows, pc) + _page_specs(page_rows, pc),
        out_specs=pl.BlockSpec((None, tq, w), per_b),
        scratch_shapes=[pltpu.VMEM((rows, dv), BF16), pltpu.VMEM((rows, page_rows), F32),
                        pltpu.VMEM((rows, 1), F32), pltpu.VMEM((rows, 1), F32), pltpu.VMEM((rows, dv), F32)])
    return pl.pallas_call(
        functools.partial(_diff_decode_body, n_pages=pc, tq=tq, lam_init=lam_init),
        grid_spec=grid_spec,
        out_shape=jax.ShapeDtypeStruct((bs, tq, w), F32),
        compiler_params=_cparams(("parallel", "arbitrary"), 58),
        name="diff_decode",
    )(page_table, lam_p, subln_g, q, k_new, v_new, *([cache_k] * pc), *([cache_v] * pc))


def _chunk_perm(page_tokens):
    n = page_tokens * G_B
    chunks = page_tokens // CMP_STRIDE
    out = np.arange(n)
    l, g, c = out // (G_B * chunks), (out // chunks) % G_B, out % chunks
    src = (c * CMP_STRIDE + l) * G_B + g
    perm = np.zeros((n, n), np.float32)
    perm[out, src] = 1.0
    return jnp.asarray(perm, BF16)


def _compress_decode_body(pt_ref, perm_ref, wk_ref, wv_ref, c_ref, *rest, n_pages):
    k_pages = rest[:n_pages]
    v_pages = rest[n_pages:2 * n_pages]
    yk_ref, yv_ref, a_scr = rest[2 * n_pages:]
    chunks = k_pages[0].shape[0] // (G_B * CMP_STRIDE)
    per_g = n_pages * chunks
    for a, (pages, w_ref, y_ref) in enumerate(((k_pages, wk_ref, yk_ref), (v_pages, wv_ref, yv_ref))):
        side_by_side = jnp.concatenate([pages[p][...].astype(BF16) for p in range(n_pages)], axis=1)
        pp = _dot(perm_ref[...], side_by_side)
        for l in range(CMP_STRIDE):
            for g in range(G_B):
                src = (l * G_B + g) * chunks
                for p in range(n_pages):
                    dst = g * per_g + p * chunks
                    a_scr[dst:dst + chunks, l * DH_B:(l + 1) * DH_B] = pp[src:src + chunks, p * DH_B:(p + 1) * DH_B]
        y = _dot(a_scr[...].astype(BF16), w_ref[...])
        y = jnp.concatenate([y[:, :DH_B] + c_ref[a, 0:1, :], y[:, DH_B:]], axis=1)
        for g in range(G_B):
            y_ref[g] = y[g * per_g:(g + 1) * per_g]


def _compress_decode(page_table, perm, wk, wv, const, cache_k, cache_v):
    bs, n_pages_total = page_table.shape
    pc = GROUP_PAGES_PER_STEP
    page_rows = cache_k.shape[1]
    chunks = page_rows // (G_B * CMP_STRIDE)
    n_chunk = n_pages_total * chunks
    const2 = lambda b, j, pt: (0, 0)
    y_spec = pl.BlockSpec((None, G_B, pc * chunks, 2 * DH_B), lambda b, j, pt: (b, 0, j, 0))
    y_shape = jax.ShapeDtypeStruct((bs, G_B, n_chunk, 2 * DH_B), F32)
    grid_spec = pltpu.PrefetchScalarGridSpec(
        num_scalar_prefetch=1,
        grid=(bs, n_pages_total // pc),
        in_specs=[pl.BlockSpec(perm.shape, const2), pl.BlockSpec(wk.shape, const2), pl.BlockSpec(wv.shape, const2),
                  pl.BlockSpec(const.shape, lambda b, j, pt: (0, 0, 0))]
        + _page_specs(page_rows, pc) + _page_specs(page_rows, pc),
        out_specs=[y_spec, y_spec],
        scratch_shapes=[pltpu.VMEM((G_B * pc * chunks, CMP_STRIDE * DH_B), F32)])
    return pl.pallas_call(
        functools.partial(_compress_decode_body, n_pages=pc),
        grid_spec=grid_spec,
        out_shape=[y_shape, y_shape],
        compiler_params=_cparams(("parallel", "arbitrary"), 40),
        name="compress_decode",
    )(page_table, perm, wk, wv, const, *([cache_k] * pc), *([cache_v] * pc))


def _nsa_decode_body(pt_ref, q_ref, yk_ref, yv_ref, selmap_ref, expand_ref, gate_ref, ksn_ref, vsn_ref, kwn_ref,
                     vwn_ref, wk_ref, wv_ref, *rest, n_pages, tq, past_len, n_sel, n_top):
    k_pages = rest[:n_pages]
    v_pages = rest[n_pages:2 * n_pages]
    o_ref, wko_ref, wvo_ref, q_scr, sel_scr, bias_scr, oc_scr, m_scr, l_scr, acc_scr = rest[2 * n_pages:]
    j = pl.program_id(1)
    rows = H_B * tq
    rows_g = HPG_B * tq
    n_keys = k_pages[0].shape[0]
    log_tq = int(math.log2(tq))
    log_g = int(math.log2(G_B))

    def group_of_row(shape):
        return lax.shift_right_logical(lax.broadcasted_iota(jnp.int32, shape, 0), int(math.log2(rows_g)))

    def qpos_of_row(shape):
        return past_len + (lax.broadcasted_iota(jnp.int32, shape, 0) & (tq - 1))

    @pl.when(j == 0)
    def _():
        q = q_ref[...]
        q_scr[...] = jnp.concatenate([q[:, h * DH_B:(h + 1) * DH_B] for h in range(H_B)], axis=0).astype(BF16)
        qpos_col = past_len + lax.broadcasted_iota(jnp.int32, (tq, 1), 0)
        for g in range(G_B):
            qg = q_scr[g * rows_g:(g + 1) * rows_g, :]
            kcmp = _compressed_kv(yk_ref[g])
            vcmp = _compressed_kv(yv_ref[g])
            n_cmp = kcmp.shape[0]
            s = _dot_nt(qg, kcmp)
            cmp_end = lax.broadcasted_iota(jnp.int32, (rows_g, n_cmp), 1) * CMP_STRIDE + (CMP_BLOCK - 1)
            p = _masked_softmax(s, cmp_end <= qpos_of_row((rows_g, n_cmp)))
            oc_scr[g * rows_g:(g + 1) * rows_g, :] = _dot(p.astype(BF16), vcmp)
            p_sum = p[0:tq]
            for h in range(1, HPG_B):
                p_sum = p_sum + p[h * tq:(h + 1) * tq]
            imp = jnp.dot(p_sum, selmap_ref[...], precision=lax.Precision.HIGHEST, preferred_element_type=F32)
            sel = _select_blocks(imp, qpos_col, n_sel, n_top)
            for h in range(HPG_B):
                sel_scr[g * rows_g + h * tq:g * rows_g + (h + 1) * tq, :] = sel
        sel_rows = sel_scr[...]
        n_lanes = sel_rows.shape[1]
        exp_slots = expand_ref.shape[1]
        blocks_per_exp = exp_slots // (SEL_BLOCK * G_B)
        slot = lax.broadcasted_iota(jnp.int32, (rows, exp_slots), 1)
        same_group = group_of_row((rows, exp_slots)) == (slot & (G_B - 1))
        for c in range(bias_scr.shape[1] // exp_slots):
            shifted = pltpu.roll(sel_rows, (n_lanes - c * blocks_per_exp) % n_lanes, 1) if c else sel_rows
            chosen = _dot(shifted[:, :LANES].astype(BF16), expand_ref[...])
            bias_scr[:, c * exp_slots:(c + 1) * exp_slots] = jnp.where((chosen > 0.5) & same_group, 0.0, NEG)
        m_scr[...] = jnp.full(m_scr.shape, NEG, F32)
        l_scr[...] = jnp.zeros(l_scr.shape, F32)
        acc_scr[...] = jnp.zeros(acc_scr.shape, F32)

    qq = q_scr[...]
    step_slots = n_pages * n_keys
    bias = bias_scr[:, pl.ds(pl.multiple_of(j * step_slots, step_slots), step_slots)]
    ss = [_dot_nt(qq, k_pages[r][...].astype(BF16)) + bias[:, r * n_keys:(r + 1) * n_keys] for r in range(n_pages)]
    vs = [v_pages[r][...].astype(BF16) for r in range(n_pages)]
    m_scr[...], l_scr[...], acc_scr[...] = _online_update_pages(ss, vs, m_scr[...], l_scr[...], acc_scr[...])

    @pl.when(j == pl.num_programs(1) - 1)
    def _():
        sel_rows = sel_scr[...]
        sel_lane = lax.broadcasted_iota(jnp.int32, sel_rows.shape, 1)

        def block_selected(blk):
            return jnp.sum(jnp.where(sel_lane == blk, sel_rows, 0.0), axis=1, keepdims=True) > 0.5

        n_new = ksn_ref.shape[0]
        c_new = lax.broadcasted_iota(jnp.int32, (rows, n_new), 1)
        t_new = lax.shift_right_logical(c_new, log_g)
        grp_ok = group_of_row((rows, n_new)) == (c_new & (G_B - 1))
        causal = (past_len + t_new) <= qpos_of_row((rows, n_new))
        ok = grp_ok & causal & block_selected(past_len // SEL_BLOCK)
        s = jnp.where(ok, _dot_nt(qq, ksn_ref[...].astype(BF16)), NEG)
        _, l, acc = _online_update(s, vsn_ref[...].astype(BF16), m_scr[...], l_scr[...], acc_scr[...])
        o_s = acc / l

        n_win = wk_ref.shape[0]
        win_tokens = n_win // G_B
        c_w = lax.broadcasted_iota(jnp.int32, (rows, n_win), 1)
        kpos = past_len - win_tokens + lax.shift_right_logical(c_w, log_g)
        dist = qpos_of_row((rows, n_win)) - kpos
        ok_w = (group_of_row((rows, n_win)) == (c_w & (G_B - 1))) & (dist >= 0) & (dist <= WINDOW) & (kpos >= 0)
        s_w = jnp.where(ok_w, _dot_nt(qq, wk_ref[...].astype(BF16)), NEG)
        dist_n = qpos_of_row((rows, n_new)) - (past_len + t_new)
        ok_n = grp_ok & (dist_n >= 0) & (dist_n <= WINDOW)
        s_n = jnp.where(ok_n, _dot_nt(qq, kwn_ref[...].astype(BF16)), NEG)
        m = jnp.maximum(jnp.max(s_w, axis=-1, keepdims=True), jnp.max(s_n, axis=-1, keepdims=True))
        e_w = jnp.where(ok_w, jnp.exp2(s_w - m), 0.0)
        e_n = jnp.where(ok_n, jnp.exp2(s_n - m), 0.0)
        l_w = jnp.sum(e_w, axis=-1, keepdims=True) + jnp.sum(e_n, axis=-1, keepdims=True)
        o_w = (_dot(e_w.astype(BF16), wv_ref[...].astype(BF16))
               + _dot(e_n.astype(BF16), vwn_ref[...].astype(BF16))) / jnp.where(l_w > 0.0, l_w, 1.0)

        gates = gate_ref[...]
        o_c = oc_scr[...]
        for head in range(H_B):
            rs = slice(head * tq, (head + 1) * tq)
            o_ref[:, head * DH_B:(head + 1) * DH_B] = (
                gates[:, 3 * head:3 * head + 1] * o_c[rs] + gates[:, 3 * head + 1:3 * head + 2] * o_s[rs]
                + gates[:, 3 * head + 2:3 * head + 3] * o_w[rs])

        keep = n_win - n_new
        wko_ref[0:keep, :] = wk_ref[n_new:n_win, :]
        wko_ref[keep:n_win, :] = kwn_ref[...]
        wvo_ref[0:keep, :] = wv_ref[n_new:n_win, :]
        wvo_ref[keep:n_win, :] = vwn_ref[...]


def _nsa_decode(page_table, q, yk, yv, selmap, gates, ks_new, vs_new, kw_new, vw_new, win_k, win_v,
                cache_k, cache_v, past_len, n_sel):
    bs, tq, w = q.shape
    n_pages_total = page_table.shape[1]
    pc = GROUP_PAGES_PER_STEP
    rows = H_B * tq
    page_rows = cache_k.shape[1]
    exp_slots = 2048
    slots_per_block = SEL_BLOCK * G_B
    expand = jnp.asarray(np.arange(LANES)[:, None] == (np.arange(exp_slots)[None, :] // slots_per_block), BF16)
    assert (n_pages_total * page_rows) % exp_slots == 0 and exp_slots // slots_per_block <= LANES
    per_b3 = lambda b, j, pt: (b, 0, 0)
    per_b4 = lambda b, j, pt: (b, 0, 0, 0)
    const2 = lambda b, j, pt: (0, 0)
    blk3 = lambda a: pl.BlockSpec((None,) + a.shape[1:], per_b3)
    y_spec = pl.BlockSpec((None,) + yk.shape[1:], per_b4)
    grid_spec = pltpu.PrefetchScalarGridSpec(
        num_scalar_prefetch=1,
        grid=(bs, n_pages_total // pc),
        in_specs=[blk3(q), y_spec, y_spec, pl.BlockSpec(selmap.shape, const2), pl.BlockSpec(expand.shape, const2),
                  blk3(gates), blk3(ks_new), blk3(vs_new), blk3(kw_new), blk3(vw_new), blk3(win_k), blk3(win_v)]
        + _page_specs(page_rows, pc) + _page_specs(page_rows, pc),
        out_specs=[blk3(q), blk3(win_k), blk3(win_v)],
        scratch_shapes=[pltpu.VMEM((rows, DH_B), BF16), pltpu.VMEM((rows, selmap.shape[1]), F32),
                        pltpu.VMEM((rows, n_pages_total * page_rows), F32),
                        pltpu.VMEM((rows, DH_B), F32), pltpu.VMEM((rows, 1), F32), pltpu.VMEM((rows, 1), F32),
                        pltpu.VMEM((rows, DH_B), F32)])
    return pl.pallas_call(
        functools.partial(_nsa_decode_body, n_pages=pc, tq=tq, past_len=past_len, n_sel=n_sel,
                          n_top=min(SEL_TOP_N, n_sel)),
        grid_spec=grid_spec,
        out_shape=[jax.ShapeDtypeStruct((bs, tq, w), F32), jax.ShapeDtypeStruct(win_k.shape, F32),
                   jax.ShapeDtypeStruct(win_v.shape, F32)],
        compiler_params=_cparams(("parallel", "arbitrary"), 48),
        name="nsa_decode",
    )(page_table, q, yk, yv, selmap, expand, gates, ks_new, vs_new, kw_new, vw_new, win_k, win_v,
      *([cache_k] * pc), *([cache_v] * pc))


def kernel(x_prompt, x_sample, cache_diff_k, cache_diff_v, cache_cmp_k, cache_cmp_v, cache_sel_k, cache_sel_v,
           cache_win_k, cache_win_v, page_table, c_prompt, c_sample, w_mod, b_mod, norm_g, w_in, w_out, lam_p,
           subln_g, cmp_pe, cmp_phi, ffn_gate, ffn_up, ffn_down):
    depth = w_mod.shape[0]
    assert depth == 1, "single-layer step"
    B, T, D = x_prompt.shape
    Bs, Ts, _ = x_sample.shape
    n_pool, page = cache_diff_k.shape[1], cache_diff_k.shape[2]
    past_len = page_table.shape[1] * page
    win_len = cache_win_k.shape[2]
    assert T % 512 == 0 and win_len <= T and win_len == WINDOW and past_len >= win_len
    assert past_len % SEL_BLOCK == 0 and Ts <= SEL_BLOCK and (Ts & (Ts - 1)) == 0
    assert (past_len + Ts - CMP_BLOCK) // CMP_STRIDE + 1 <= past_len // CMP_STRIDE
    lam_init = 0.8 - 0.6 * math.exp(-0.3 * 0)
    Np, Ns = B * T, Bs * Ts
    gq = G_B * DH_B

    wg, wu, wd = ffn_gate[0].astype(BF16), ffn_up[0].astype(BF16), ffn_down[0].astype(BF16)
    w_pad = jnp.pad(w_in[0].astype(BF16), ((0, 0), (0, N_IN_BLOCKS * IN_BLOCK - w_in.shape[2])))
    w_out_a, w_out_b = w_out[0, :H_A * 2 * DH_A].astype(BF16), w_out[0, H_A * 2 * DH_A:].astype(BF16)
    ng = norm_g[0].reshape(6, 1, D)
    wck, pek = _compress_weights(cmp_pe[0, 0], cmp_phi[0, 0])
    wcv, pev = _compress_weights(cmp_pe[0, 1], cmp_phi[0, 1])
    cmp_const = _cmp_const(pek, pev, wck, wcv)

    n_c = B + Bs
    c_all = jnp.pad(jnp.concatenate([c_prompt, c_sample], axis=0), ((0, -n_c % 8), (0, 0)))
    mod = _adaln_mod(c_all, w_mod[0], b_mod[0][None]).reshape(c_all.shape[0], 9, D)
    mod_p = mod[:B].transpose(1, 0, 2).reshape(9, B, 1, D)
    mod_s = jnp.repeat(mod[B:n_c].transpose(1, 0, 2), Ts, axis=1)

    def split_states(pr, batch, seq):
        return (pr["k_a"].reshape(1, batch, seq, H_A, 2 * DH_A), pr["v_a"].reshape(1, batch, seq, H_A, 2 * DH_A),
                pr["kc"].reshape(1, batch, seq, G_B, DH_B), pr["vc"].reshape(1, batch, seq, G_B, DH_B),
                pr["ks"].reshape(1, batch, seq, G_B, DH_B), pr["vs"].reshape(1, batch, seq, G_B, DH_B))

    names = ("q_a", "k_a", "k_a_bf", "v_a", "v_a_bf", "q_b", "kc", "vc", "ks", "vs", "kw", "vw", "nkv", "gates")

    tm_p = 512
    xp = x_prompt.reshape(Np, D)
    x1 = _ffn(xp, mod_p, (0, 1, 2), T, ng[0], ng[1], wg, wu, wd, 0, tm_p, FFN_TF)
    pr = dict(zip(names, _inproj(x1, mod_p, 3, 4, T, ng[2], w_pad, _rope_tables(np.arange(T)), tm_p)))
    o_a = _diff_prompt(pr["q_a"], pr["k_a_bf"], pr["v_a_bf"], lam_p[0], subln_g[0][None], B, T, lam_init)
    yk, yv = _compress_prompt(pr["kc"], pr["vc"], wck, wcv, cmp_const, B, T)
    n_cmp_p = (T - CMP_BLOCK) // CMP_STRIDE + 1
    n_sel_p = -(-T // SEL_BLOCK)
    assert n_sel_p <= LANES
    selmap_p = _sel_map(T // CMP_STRIDE, n_cmp_p, n_sel_p, n_sel_p).T
    o_b = _nsa_prompt(pr["q_b"], yk, yv, pr["nkv"], pr["gates"], selmap_p, B, T)
    x2 = _outproj(o_a, o_b, w_out_a, w_out_b, x1, mod_p, 5, T, ng[3], tm_p)
    y_p = _ffn(x2, mod_p, (6, 7, 8), T, ng[4], ng[5], wg, wu, wd, 1, tm_p, FFN_TF).reshape(B, T, D)
    st_p = split_states(pr, B, T)
    wk_p = pr["kw"].reshape(1, B, T, G_B, DH_B)[:, :, T - win_len:]
    wv_p = pr["vw"].reshape(1, B, T, G_B, DH_B)[:, :, T - win_len:]

    xs = x_sample.reshape(Ns, D)
    x1s = _ffn(xs, mod_s, (0, 1, 2), 1, ng[0], ng[1], wg, wu, wd, 0, Ns, FFN_TF)
    pos_s = np.tile(past_len + np.arange(Ts), Bs)
    ps = dict(zip(names, _inproj(x1s, mod_s, 3, 4, 1, ng[2], w_pad, _rope_tables(pos_s), Ns)))
    ck = cache_diff_k[0].reshape(n_pool, page * H_A, 2 * DH_A)
    cv = cache_diff_v[0].reshape(n_pool, page * H_A, 2 * DH_A)
    o_as = _diff_decode(page_table, lam_p[0], subln_g[0][None],
                        ps["q_a"].astype(F32).reshape(Bs, Ts, H_A * 2 * DH_A),
                        ps["k_a"].reshape(Bs, Ts * H_A, 2 * DH_A), ps["v_a"].reshape(Bs, Ts * H_A, 2 * DH_A),
                        ck, cv, lam_init)
    pool3 = lambda c: c[0].reshape(n_pool, page * G_B, DH_B)
    yks, yvs = _compress_decode(page_table, _chunk_perm(page), wck, wcv, cmp_const,
                                pool3(cache_cmp_k), pool3(cache_cmp_v))
    tk_s = past_len + Ts
    n_cmp_s = (tk_s - CMP_BLOCK) // CMP_STRIDE + 1
    n_sel_s = -(-tk_s // SEL_BLOCK)
    selmap_s = _sel_map(past_len // CMP_STRIDE, n_cmp_s, n_sel_s, -(-n_sel_s // LANES) * LANES)
    new3 = lambda a: a.reshape(Bs, Ts * G_B, DH_B)
    o_bs, wk_s, wv_s = _nsa_decode(
        page_table, ps["q_b"].astype(F32).reshape(Bs, Ts, H_B * DH_B), yks, yvs, selmap_s,
        ps["gates"].reshape(Bs, Ts, LANES), new3(ps["ks"]), new3(ps["vs"]), new3(ps["kw"]), new3(ps["vw"]),
        cache_win_k[0].reshape(Bs, win_len * G_B, DH_B), cache_win_v[0].reshape(Bs, win_len * G_B, DH_B),
        pool3(cache_sel_k), pool3(cache_sel_v), past_len, n_sel_s)
    x2s = _outproj(o_as.reshape(Ns, -1).astype(BF16), o_bs.reshape(Ns, -1).astype(BF16), w_out_a, w_out_b,
                   x1s, mod_s, 5, 1, ng[3], Ns)
    y_s = _ffn(x2s, mod_s, (6, 7, 8), 1, ng[4], ng[5], wg, wu, wd, 1, Ns, FFN_TF).reshape(Bs, Ts, D)
    st_s = split_states(ps, Bs, Ts)
    wk_s = wk_s.reshape(1, Bs, win_len, G_B, DH_B)
    wv_s = wv_s.reshape(1, Bs, win_len, G_B, DH_B)

    return (y_p, y_s) + st_p + (wk_p, wv_p) + st_s + (wk_s, wv_s)
```

```python
import functools
import math

import numpy as np
import jax
import jax.numpy as jnp
from jax import lax
from jax.experimental import pallas as pl
from jax.experimental.pallas import tpu as pltpu

F32 = jnp.float32
BF16 = jnp.bfloat16

DH_A = 64
H_A = 8
DH_B = 128
H_B = 8
G_B = 2
HPG_B = H_B // G_B
CMP_BLOCK = 32
CMP_STRIDE = 16
SEL_BLOCK = 64
SEL_TOP_N = 16
N_LOCAL = 2
WINDOW = 512
ROPE_THETA = 10000.0
EPS = 1e-6
NEG = -1e30
FORCE = 1e4
LOG2E = math.log2(math.e)

LANES = 128
MIB = 1024 * 1024
PAGES_PER_STEP = 16
GROUP_PAGES_PER_STEP = 32
FFN_TF = 1024


def _dot(a, b):
    return jnp.dot(a, b, preferred_element_type=F32)


def _dot_nt(a, b):
    return lax.dot_general(a, b, (((1,), (1,)), ((), ())), preferred_element_type=F32)


def _rms(x, g):
    return x * lax.rsqrt(jnp.mean(x * x, axis=-1, keepdims=True) + EPS) * g


def _cparams(sem, vmem_mib):
    return pltpu.CompilerParams(dimension_semantics=sem, vmem_limit_bytes=vmem_mib * MIB)


def _mod_body(c_ref, w_ref, b_ref, o_ref):
    c = c_ref[...]
    a = (c * jax.nn.sigmoid(c)).astype(BF16)
    o_ref[...] = _dot(a, w_ref[...].astype(BF16)) + b_ref[...]


def _adaln_mod(c, w_mod, b_mod):
    R, D = c.shape
    n_out = w_mod.shape[1]
    tn = 1024
    return pl.pallas_call(
        _mod_body,
        grid=(n_out // tn,),
        in_specs=[pl.BlockSpec((R, D), lambda j: (0, 0)),
                  pl.BlockSpec((D, tn), lambda j: (0, j)),
                  pl.BlockSpec((1, tn), lambda j: (0, j))],
        out_specs=pl.BlockSpec((R, tn), lambda j: (0, j)),
        out_shape=jax.ShapeDtypeStruct((R, n_out), F32),
        compiler_params=_cparams(("arbitrary",), 40),
        name="adaln_mod",
    )(c, w_mod, b_mod)


def _mod_spec(mod, k, tm, rows_per_group):
    if mod.ndim == 4:
        return pl.BlockSpec((None, None, 1, mod.shape[-1]),
                            lambda i, *_, k=k: (k, (i * tm) // rows_per_group, 0, 0))
    return pl.BlockSpec((None, tm, mod.shape[-1]), lambda i, *_, k=k: (k, i, 0))


def _ffn_body(x_ref, sh_ref, sc_ref, gt_ref, gpre_ref, gpost_ref, wg_ref, wu_ref, wd_ref, o_ref,
              h_scr, acc_scr, *, n_chunk):
    f = pl.program_id(1)

    @pl.when(f == 0)
    def _():
        h = _rms(x_ref[...], gpre_ref[...]) * (1.0 + sc_ref[...]) + sh_ref[...]
        h_scr[...] = h.astype(BF16)
        acc_scr[...] = jnp.zeros(acc_scr.shape, F32)

    h = h_scr[...]
    tf = wg_ref.shape[1]
    hw = min(tf, 512)
    parts = []
    for c in range(tf // hw):
        g = _dot(h, wg_ref[:, c * hw:(c + 1) * hw])
        u = _dot(h, wu_ref[:, c * hw:(c + 1) * hw])
        parts.append((g * jax.nn.sigmoid(g) * u).astype(BF16))
    a = parts[0] if len(parts) == 1 else jnp.concatenate(parts, axis=1)
    d = acc_scr.shape[1]
    cw = d // n_chunk
    for n in range(n_chunk):
        acc_scr[:, n * cw:(n + 1) * cw] += _dot(a, wd_ref[:, n * cw:(n + 1) * cw])

    @pl.when(f == pl.num_programs(1) - 1)
    def _():
        o_ref[...] = x_ref[...] + 0.5 * gt_ref[...] * _rms(acc_scr[...], gpost_ref[...])


def _ffn(x, mod, ks, rows_per_group, g_pre, g_post, wg, wu, wd, layer, tm, tf):
    n, d = x.shape
    dff = wg.shape[2]
    row = lambda i, f: (i, 0)
    const = lambda i, f: (0, 0)
    return pl.pallas_call(
        functools.partial(_ffn_body, n_chunk=4),
        grid=(n // tm, dff // tf),
        in_specs=[pl.BlockSpec((tm, d), row),
                  _mod_spec(mod, ks[0], tm, rows_per_group),
                  _mod_spec(mod, ks[1], tm, rows_per_group),
                  _mod_spec(mod, ks[2], tm, rows_per_group),
                  pl.BlockSpec((1, d), const), pl.BlockSpec((1, d), const),
                  pl.BlockSpec((None, d, tf), lambda i, f: (layer, 0, f)),
                  pl.BlockSpec((None, d, tf), lambda i, f: (layer, 0, f)),
                  pl.BlockSpec((None, tf, d), lambda i, f: (layer, f, 0))],
        out_specs=pl.BlockSpec((tm, d), row),
        out_shape=jax.ShapeDtypeStruct((n, d), F32),
        scratch_shapes=[pltpu.VMEM((tm, d), BF16), pltpu.VMEM((tm, d), F32)],
        compiler_params=_cparams(("parallel", "arbitrary"), 60),
        name="ffn",
    )(x, mod, mod, mod, g_pre, g_post, wg, wu, wd)


IN_BLOCK = 1024
N_IN_BLOCKS = 6


def _inproj_body(x_ref, sh_ref, sc_ref, g_ref, w_ref, ca_ref, sa_ref, cb_ref, sb_ref,
                 qa_ref, ka_ref, kab_ref, va_ref, vab_ref, qb_ref, kc_ref, vc_ref, ks_ref, vs_ref,
                 kw_ref, vw_ref, nkv_ref, gate_ref, h_scr):
    j = pl.program_id(1)

    @pl.when(j == 0)
    def _():
        h = _rms(x_ref[...], g_ref[...]) * (1.0 + sc_ref[...]) + sh_ref[...]
        h_scr[...] = h.astype(BF16)

    lane = lax.broadcasted_iota(jnp.int32, (1, LANES), 1)
    low32 = (lane & (DH_A - 1)) < (DH_A // 2)

    def cols(c):
        y = _dot(h_scr[...], w_ref[:, c * 256:(c + 1) * 256])
        return y[:, :LANES], y[:, LANES:]

    def rope64(y):
        partner = jnp.where(low32, pltpu.roll(y, LANES - DH_A // 2, 1), pltpu.roll(y, DH_A // 2, 1))
        return y * ca_ref[...] + partner * sa_ref[...]

    def rope128(y):
        return y * cb_ref[...] + pltpu.roll(y, DH_B // 2, 1) * sb_ref[...]

    def tile(t):
        return slice(t * LANES, (t + 1) * LANES)

    @pl.when(j == 0)
    def _():
        for c in range(4):
            for t, y in enumerate(cols(c)):
                qa_ref[:, tile(2 * c + t)] = (rope64(y) * (DH_A ** -0.5 * LOG2E)).astype(BF16)

    @pl.when(j == 1)
    def _():
        for c in range(4):
            for t, y in enumerate(cols(c)):
                r = rope64(y)
                ka_ref[:, tile(2 * c + t)] = r
                kab_ref[:, tile(2 * c + t)] = r.astype(BF16)

    @pl.when(j == 2)
    def _():
        for c in range(4):
            for t, y in enumerate(cols(c)):
                va_ref[:, tile(2 * c + t)] = y
                vab_ref[:, tile(2 * c + t)] = y.astype(BF16)

    @pl.when(j == 3)
    def _():
        for c in range(4):
            for t, y in enumerate(cols(c)):
                qb_ref[:, tile(2 * c + t)] = (rope128(y) * (DH_B ** -0.5 * LOG2E)).astype(BF16)

    tm = x_ref.shape[0]

    def group_rows(g):
        return pl.ds(g, tm, stride=G_B)

    @pl.when(j == 4)
    def _():
        for c, (ref, roped) in enumerate(((kc_ref, True), (vc_ref, False), (ks_ref, True), (vs_ref, False))):
            for t, y in enumerate(cols(c)):
                r = rope128(y) if roped else y
                ref[group_rows(t), :] = r
                if c >= 2:
                    nkv_ref[:, tile(2 * (c - 2) + t)] = r.astype(BF16)

    @pl.when(j == 5)
    def _():
        for c, (ref, roped) in enumerate(((kw_ref, True), (vw_ref, False))):
            for t, y in enumerate(cols(c)):
                r = rope128(y) if roped else y
                ref[group_rows(t), :] = r
                nkv_ref[:, tile(4 + 2 * c + t)] = r.astype(BF16)
        gate_ref[...] = jax.nn.sigmoid(cols(2)[0])


def _inproj(x, mod, k_shift, k_scale, rows_per_group, g, w_pad, tabs, tm):
    n, d = x.shape
    gq = G_B * DH_B
    n_tab = tabs[0].shape[0] // tm
    row = lambda i, j: (i, 0)
    const = lambda i, j: (0, 0)
    tab = pl.BlockSpec((tm, LANES), lambda i, j: (i % n_tab, 0))
    wide = lambda dt: jax.ShapeDtypeStruct((n, IN_BLOCK), dt)
    grouped = jax.ShapeDtypeStruct((n * G_B, DH_B), F32)
    out_shape = [wide(BF16), wide(F32), wide(BF16), wide(F32), wide(BF16), wide(BF16)] + [grouped] * 6 + [
        wide(BF16), jax.ShapeDtypeStruct((n, LANES), F32)]
    first_step = [0, 1, 1, 2, 2, 3, 4, 4, 4, 4, 5, 5, 4, 5]

    def lagged(k):
        return lambda i, j: (jnp.where(j >= k, i, jnp.maximum(i - 1, 0)), 0)

    out_specs = [pl.BlockSpec((tm * s.shape[0] // n, s.shape[1]), lagged(k)) for s, k in zip(out_shape, first_step)]
    return pl.pallas_call(
        _inproj_body,
        grid=(n // tm, N_IN_BLOCKS),
        in_specs=[pl.BlockSpec((tm, d), row),
                  _mod_spec(mod, k_shift, tm, rows_per_group),
                  _mod_spec(mod, k_scale, tm, rows_per_group),
                  pl.BlockSpec((1, d), const),
                  pl.BlockSpec((d, IN_BLOCK), lambda i, j: (0, j)),
                  tab, tab, tab, tab],
        out_specs=out_specs,
        out_shape=out_shape,
        scratch_shapes=[pltpu.VMEM((tm, d), BF16)],
        compiler_params=_cparams(("arbitrary", "arbitrary"), 56),
        name="inproj",
    )(x, mod, mod, g, w_pad, *tabs)


def _rope_tables(pos):
    pos = np.asarray(pos, np.float64)[:, None]

    def tables(head_dim):
        half = head_dim // 2
        inv = ROPE_THETA ** (-np.arange(half, dtype=np.float64) / half)
        ang = pos * inv[None, :]
        cos = np.concatenate([np.cos(ang), np.cos(ang)], axis=1)
        sin = np.concatenate([-np.sin(ang), np.sin(ang)], axis=1)
        rep = LANES // head_dim
        return (jnp.asarray(np.tile(cos, (1, rep)), F32), jnp.asarray(np.tile(sin, (1, rep)), F32))

    ca, sa = tables(DH_A)
    cb, sb = tables(DH_B)
    return ca, sa, cb, sb


def _lambda(lamp_ref, lam_init):
    lp = lamp_ref[...]
    a = jnp.sum(lp[0:1] * lp[1:2], axis=1, keepdims=True)
    b = jnp.sum(lp[2:3] * lp[3:4], axis=1, keepdims=True)
    return jnp.exp(a) - jnp.exp(b) + lam_init


def _online_update(s, v, m, l, acc):
    m_new = jnp.maximum(m, jnp.max(s, axis=-1, keepdims=True))
    p = jnp.exp2(s - m_new)
    alpha = jnp.exp2(m - m_new)
    l = alpha * l + jnp.sum(p, axis=-1, keepdims=True)
    acc = alpha * acc + _dot(p.astype(BF16), v)
    return m_new, l, acc


def _online_update_pages(ss, vs, m, l, acc):
    m_blk = jnp.max(ss[0], axis=-1, keepdims=True)
    for s in ss[1:]:
        m_blk = jnp.maximum(m_blk, jnp.max(s, axis=-1, keepdims=True))
    m_new = jnp.maximum(m, m_blk)
    alpha = jnp.exp2(m - m_new)
    l = alpha * l
    acc = alpha * acc
    for s, v in zip(ss, vs):
        p = jnp.exp2(s - m_new)
        l = l + jnp.sum(p, axis=-1, keepdims=True)
        acc = acc + _dot(p.astype(BF16), v)
    return m_new, l, acc


def _softmax_init(rows, dv):
    return (jnp.full((rows, 1), NEG, F32), jnp.zeros((rows, 1), F32), jnp.zeros((rows, dv), F32))


def _masked_softmax(s, mask):
    sm = jnp.where(mask, s, NEG)
    m = jnp.max(sm, axis=-1, keepdims=True)
    e = jnp.where(mask, jnp.exp2(sm - m), 0.0)
    l = jnp.sum(e, axis=-1, keepdims=True)
    return e / jnp.where(l > 0.0, l, 1.0)


def _select_blocks(imp, qpos, n_blocks, n_top):
    shape = imp.shape
    blk = lax.broadcasted_iota(jnp.int32, shape, 1)
    rel = lax.shift_right_logical(qpos, int(math.log2(SEL_BLOCK))) - blk
    visible = rel >= 0
    forced = (blk == 0) | (visible & (rel < N_LOCAL))
    score = jnp.where(visible, jnp.where(forced, FORCE + imp, imp), NEG)
    rank = jnp.zeros(shape, F32)
    for i in range(n_blocks):
        col = score[:, i:i + 1]
        beats = (col > score) | ((col == score) & (blk > i))
        rank = rank + jnp.where(beats, 1.0, 0.0)
    return jnp.where((rank < n_top) & visible, 1.0, 0.0)


def _select_blocks_t(imp_t, qpos_row, n_blocks, n_top):
    shape = imp_t.shape
    blk = lax.broadcasted_iota(jnp.int32, shape, 0)
    rel = lax.shift_right_logical(qpos_row, int(math.log2(SEL_BLOCK))) - blk
    visible = rel >= 0
    forced = (blk == 0) | (visible & (rel < N_LOCAL))
    score = jnp.where(visible, jnp.where(forced, FORCE + imp_t, imp_t), NEG)
    rank = jnp.zeros(shape, F32)
    for i in range(n_blocks):
        row = score[i:i + 1, :]
        beats = (row > score) | ((row == score) & (blk > i))
        rank = rank + jnp.where(beats, 1.0, 0.0)
    return jnp.where((rank < n_top) & visible, 1.0, 0.0)


def _compressed_kv(y):
    n = y.shape[0]
    return (y[:, :DH_B] + pltpu.roll(y[:, DH_B:], n - 1, 0)).astype(BF16)


def _diff_prompt_body(lamp_ref, g_ref, q_ref, k_ref, v_ref, o_ref, *, tq, lam_init):
    qi = pl.program_id(2)
    lam = _lambda(lamp_ref, lam_init)
    q = q_ref[...]
    lane = lax.broadcasted_iota(jnp.int32, (1, 2 * DH_A), 1)
    zero = jnp.zeros_like(q)
    qq = jnp.concatenate([jnp.where(lane < DH_A, q, zero), jnp.where(lane >= DH_A, q, zero)], axis=0)

    def chunk(j):
        start = pl.multiple_of(j * tq, tq)
        return k_ref[pl.ds(start, tq), :], v_ref[pl.ds(start, tq), :]

    def body(j, carry):
        k, v = chunk(j)
        return _online_update(_dot_nt(qq, k), v, *carry)

    carry = lax.fori_loop(0, qi, body, _softmax_init(2 * tq, 2 * DH_A))
    k, v = chunk(qi)
    s = _dot_nt(qq, k)
    r = lax.broadcasted_iota(jnp.int32, s.shape, 0) & (tq - 1)
    c = lax.broadcasted_iota(jnp.int32, s.shape, 1)
    m, l, acc = _online_update(jnp.where(c <= r, s, NEG), v, *carry)
    o = acc / l
    o_ref[...] = (_rms(o[:tq] - lam * o[tq:], g_ref[...]) * (1.0 - lam_init)).astype(o_ref.dtype)


def _diff_prompt(q, k, v, lam_p, subln_g, batch, seq, lam_init, tq=512):
    n, w = q.shape
    dv = 2 * DH_A
    nq = seq // tq
    return pl.pallas_call(
        functools.partial(_diff_prompt_body, tq=tq, lam_init=lam_init),
        grid=(batch, H_A, nq),
        in_specs=[pl.BlockSpec(lam_p.shape, lambda b, h, i: (0, 0)),
                  pl.BlockSpec((1, dv), lambda b, h, i: (0, 0)),
                  pl.BlockSpec((tq, dv), lambda b, h, i: (b * nq + i, h)),
                  pl.BlockSpec((seq, dv), lambda b, h, i: (b, h)),
                  pl.BlockSpec((seq, dv), lambda b, h, i: (b, h))],
        out_specs=pl.BlockSpec((tq, dv), lambda b, h, i: (b * nq + i, h)),
        out_shape=jax.ShapeDtypeStruct((n, w), BF16),
        compiler_params=_cparams(("parallel", "parallel", "arbitrary"), 48),
        name="diff_prompt",
    )(lam_p, subln_g, q, k, v)


def _cmp_const_body(pek_ref, pev_ref, wk_ref, wv_ref, o_ref):
    for a, (pe_ref, w_ref) in enumerate(((pek_ref, wk_ref), (pev_ref, wv_ref))):
        cw = _dot(pe_ref[...].astype(BF16), w_ref[...])
        o_ref[a] = jnp.broadcast_to(cw[0:1, :DH_B] + cw[1:2, DH_B:], (8, DH_B))


def _cmp_const(pek, pev, wk, wv):
    return pl.pallas_call(
        _cmp_const_body,
        out_shape=jax.ShapeDtypeStruct((2, 8, DH_B), F32),
        name="cmp_const",
    )(pek, pev, wk, wv)


def _compress_weights(pe, phi):
    half = CMP_BLOCK // 2
    w = jnp.concatenate([phi[:half].reshape(half * DH_B, DH_B), phi[half:].reshape(half * DH_B, DH_B)], axis=1)
    pe2 = jnp.zeros((8, half * DH_B), F32)
    pe2 = pe2.at[0].set(pe[:half].reshape(-1)).at[1].set(pe[half:].reshape(-1))
    return w.astype(BF16), pe2


def _compress_prompt_body(rk_ref, rv_ref, wk_ref, wv_ref, c_ref, yk_ref, yv_ref):
    half = CMP_BLOCK // 2
    for a, (r_ref, w_ref, y_ref) in enumerate(((rk_ref, wk_ref, yk_ref), (rv_ref, wv_ref, yv_ref))):
        parts = []
        for g in range(G_B):
            parts.append(jnp.concatenate(
                [r_ref[:, (l * G_B + g) * DH_B:(l * G_B + g + 1) * DH_B] for l in range(half)], axis=1))
        a_mat = jnp.concatenate(parts, axis=0).astype(BF16)
        y = _dot(a_mat, w_ref[...])
        y = jnp.concatenate([y[:, :DH_B] + c_ref[a, 0:1, :], y[:, DH_B:]], axis=1)
        n_chunk = y.shape[0] // G_B
        for g in range(G_B):
            y_ref[g] = y[g * n_chunk:(g + 1) * n_chunk]


def _compress_prompt(kc, vc, wk, wv, const, batch, seq):
    n_chunk = seq // CMP_STRIDE
    width = CMP_STRIDE * G_B * DH_B
    rk = kc.reshape(batch, n_chunk, width)
    rv = vc.reshape(batch, n_chunk, width)
    r_spec = pl.BlockSpec((None, n_chunk, width), lambda b: (b, 0, 0))
    w_spec = pl.BlockSpec(wk.shape, lambda b: (0, 0))
    y_spec = pl.BlockSpec((None, G_B, n_chunk, 2 * DH_B), lambda b: (b, 0, 0, 0))
    y_shape = jax.ShapeDtypeStruct((batch, G_B, n_chunk, 2 * DH_B), F32)
    return pl.pallas_call(
        _compress_prompt_body,
        grid=(batch,),
        in_specs=[r_spec, r_spec, w_spec, w_spec, pl.BlockSpec(const.shape, lambda b: (0, 0, 0))],
        out_specs=[y_spec, y_spec],
        out_shape=[y_shape, y_shape],
        compiler_params=_cparams(("parallel",), 40),
        name="compress_prompt",
    )(rk, rv, wk, wv, const)


def _nsa_prompt_body(q_ref, yk_ref, yv_ref, kv_ref, gate_ref, selmap_ref, o_ref, *, tq, n_sel, n_top):
    qi = pl.program_id(1)
    rows = HPG_B * tq
    gq = G_B * DH_B
    q0 = qi * tq
    qpos_col = q0 + lax.broadcasted_iota(jnp.int32, (tq, 1), 0)
    qpos_row = q0 + lax.broadcasted_iota(jnp.int32, (1, tq), 1)

    def stack_rows(x):
        return jnp.concatenate([x] * HPG_B, axis=0)

    gates = gate_ref[...]
    for g in range(G_B):
        qg = jnp.concatenate([q_ref[:, (g * HPG_B + h) * DH_B:(g * HPG_B + h + 1) * DH_B]
                              for h in range(HPG_B)], axis=0)

        kcmp = _compressed_kv(yk_ref[g])
        vcmp = _compressed_kv(yv_ref[g])
        n_cmp = kcmp.shape[0]
        s = _dot_nt(qg, kcmp)
        cmp_end = lax.broadcasted_iota(jnp.int32, (tq, n_cmp), 1) * CMP_STRIDE + (CMP_BLOCK - 1)
        p = _masked_softmax(s, stack_rows(jnp.where(cmp_end <= qpos_col, 1.0, 0.0)) > 0.5)
        o_c = _dot(p.astype(BF16), vcmp)
        p_sum = p[0:tq]
        for h in range(1, HPG_B):
            p_sum = p_sum + p[h * tq:(h + 1) * tq]
        imp_t = lax.dot_general(selmap_ref[...], p_sum, (((1,), (1,)), ((), ())),
                                precision=lax.Precision.HIGHEST, preferred_element_type=F32)
        sel_t = _select_blocks_t(imp_t, qpos_row, n_sel, n_top)
        sel = jnp.concatenate([sel_t, jnp.zeros((LANES - n_sel, tq), F32)], axis=0).T.astype(BF16)

        tk = 2 * tq
        blk_of_key = lax.shift_right_logical(lax.broadcasted_iota(jnp.int32, (LANES, tk), 1),
                                             int(math.log2(SEL_BLOCK)))
        blk_row = lax.broadcasted_iota(jnp.int32, (LANES, tk), 0)
        key_col_s = lax.broadcasted_iota(jnp.int32, (tq, tk), 1)

        def add_bias(s, bias):
            return (s.reshape(HPG_B, tq, s.shape[1]) + bias[None]).reshape(s.shape)

        def sel_step(j, carry):
            start = pl.multiple_of(j * tk, tk)
            k = kv_ref[pl.ds(start, tk), g * DH_B:(g + 1) * DH_B]
            v = kv_ref[pl.ds(start, tk), gq + g * DH_B:gq + (g + 1) * DH_B]
            expand = jnp.where(blk_row == blk_of_key + j * (tk // SEL_BLOCK), 1.0, 0.0).astype(BF16)
            ok = (_dot(sel, expand) > 0.5) & (key_col_s + j * tk <= qpos_col)
            return _online_update(add_bias(_dot_nt(qg, k), jnp.where(ok, 0.0, NEG)), v, *carry)

        n_sel_steps = lax.shift_right_logical(qi + 2, 1)
        _, l, acc = lax.fori_loop(0, n_sel_steps, sel_step, _softmax_init(rows, DH_B))
        o_s = acc / l

        span = tq + WINDOW
        w_start = pl.multiple_of(jnp.maximum(q0 - WINDOW, 0), tq)
        k = kv_ref[pl.ds(w_start, span), 2 * gq + g * DH_B:2 * gq + (g + 1) * DH_B]
        v = kv_ref[pl.ds(w_start, span), 3 * gq + g * DH_B:3 * gq + (g + 1) * DH_B]
        dist = qpos_col - (w_start + lax.broadcasted_iota(jnp.int32, (tq, span), 1))
        s = add_bias(_dot_nt(qg, k), jnp.where((dist >= 0) & (dist <= WINDOW), 0.0, NEG))
        p = jnp.exp2(s - jnp.max(s, axis=-1, keepdims=True))
        o_w = _dot(p.astype(BF16), v) / jnp.sum(p, axis=-1, keepdims=True)

        for h in range(HPG_B):
            head = g * HPG_B + h
            rs = slice(h * tq, (h + 1) * tq)
            o = (gates[:, 3 * head:3 * head + 1] * o_c[rs] + gates[:, 3 * head + 1:3 * head + 2] * o_s[rs]
                 + gates[:, 3 * head + 2:3 * head + 3] * o_w[rs])
            o_ref[:, head * DH_B:(head + 1) * DH_B] = o.astype(o_ref.dtype)


def _nsa_prompt(q, yk, yv, nkv, gates, selmap, batch, seq, tq=256):
    n, w = q.shape
    nq = seq // tq
    assert seq % (2 * tq) == 0 and WINDOW % tq == 0 and seq >= tq + WINDOW
    n_sel = -(-seq // SEL_BLOCK)
    n_chunk = yk.shape[2]
    y_spec = pl.BlockSpec((None, G_B, n_chunk, 2 * DH_B), lambda b, i: (b, 0, 0, 0))
    return pl.pallas_call(
        functools.partial(_nsa_prompt_body, tq=tq, n_sel=n_sel, n_top=min(SEL_TOP_N, n_sel)),
        grid=(batch, nq),
        in_specs=[pl.BlockSpec((tq, w), lambda b, i: (b * nq + i, 0)),
                  y_spec, y_spec,
                  pl.BlockSpec((seq, nkv.shape[1]), lambda b, i: (b, 0)),
                  pl.BlockSpec((tq, LANES), lambda b, i: (b * nq + i, 0)),
                  pl.BlockSpec(selmap.shape, lambda b, i: (0, 0))],
        out_specs=pl.BlockSpec((tq, w), lambda b, i: (b * nq + i, 0)),
        out_shape=jax.ShapeDtypeStruct((n, w), BF16),
        compiler_params=_cparams(("parallel", "arbitrary"), 48),
        name="nsa_prompt",
    )(q, yk, yv, nkv, gates, selmap)


def _sel_map(n_chunk, n_cmp, n_sel, n_sel_pad):
    i = np.arange(n_chunk)[:, None] * CMP_STRIDE
    j = np.arange(n_sel_pad)[None, :] * SEL_BLOCK
    m = (i < j + SEL_BLOCK) & (i + CMP_BLOCK > j)
    m &= (np.arange(n_chunk)[:, None] < n_cmp) & (np.arange(n_sel_pad)[None, :] < n_sel)
    return jnp.asarray(m.astype(np.float32))


def _outproj_body(oa_ref, ob_ref, wa_ref, wb_ref, x_ref, gt_ref, g_ref, o_ref):
    m = _dot(oa_ref[...], wa_ref[...]) + _dot(ob_ref[...], wb_ref[...])
    o_ref[...] = x_ref[...] + gt_ref[...] * _rms(m, g_ref[...])


def _outproj(o_a, o_b, w_a, w_b, x, mod, k_gate, rows_per_group, g, tm):
    n, d = x.shape
    row = lambda i: (i, 0)
    const = lambda i: (0, 0)
    return pl.pallas_call(
        _outproj_body,
        grid=(n // tm,),
        in_specs=[pl.BlockSpec((tm, o_a.shape[1]), row), pl.BlockSpec((tm, o_b.shape[1]), row),
                  pl.BlockSpec(w_a.shape, const), pl.BlockSpec(w_b.shape, const),
                  pl.BlockSpec((tm, d), row),
                  _mod_spec(mod, k_gate, tm, rows_per_group),
                  pl.BlockSpec((1, d), const)],
        out_specs=pl.BlockSpec((tm, d), row),
        out_shape=jax.ShapeDtypeStruct((n, d), F32),
        compiler_params=_cparams(("parallel",), 48),
        name="outproj",
    )(o_a, o_b, w_a, w_b, x, mod, g)


def _page_specs(rows, n_pages):
    return [pl.BlockSpec((None, rows, LANES), lambda b, j, pt, r=r: (pt[b, j * n_pages + r], 0, 0))
            for r in range(n_pages)]


def _diff_decode_body(pt_ref, lamp_ref, g_ref, q_ref, knew_ref, vnew_ref, *rest, n_pages, tq, lam_init):
    k_pages = rest[:n_pages]
    v_pages = rest[n_pages:2 * n_pages]
    o_ref, q_scr, bias_scr, m_scr, l_scr, acc_scr = rest[2 * n_pages:]
    j = pl.program_id(1)
    rows = H_A * 2 * tq
    n_keys = k_pages[0].shape[0]

    @pl.when(j == 0)
    def _():
        q = q_ref[...]
        lane = lax.broadcasted_iota(jnp.int32, (1, 2 * DH_A), 1)
        parts = []
        for h in range(H_A):
            qh = q[:, h * 2 * DH_A:(h + 1) * 2 * DH_A]
            parts += [jnp.where(lane < DH_A, qh, 0.0), jnp.where(lane >= DH_A, qh, 0.0)]
        q_scr[...] = jnp.concatenate(parts, axis=0).astype(BF16)
        head_of_row = lax.shift_right_logical(lax.broadcasted_iota(jnp.int32, (rows, n_keys), 0),
                                              int(math.log2(2 * tq)))
        head_of_key = lax.broadcasted_iota(jnp.int32, (rows, n_keys), 1) & (H_A - 1)
        bias_scr[...] = jnp.where(head_of_row == head_of_key, 0.0, NEG)
        m_scr[...] = jnp.full(m_scr.shape, NEG, F32)
        l_scr[...] = jnp.zeros(l_scr.shape, F32)
        acc_scr[...] = jnp.zeros(acc_scr.shape, F32)

    qq = q_scr[...]
    ss = [_dot_nt(qq, k_pages[r][...].astype(BF16)) + bias_scr[...] for r in range(n_pages)]
    vs = [v_pages[r][...].astype(BF16) for r in range(n_pages)]
    m_scr[...], l_scr[...], acc_scr[...] = _online_update_pages(ss, vs, m_scr[...], l_scr[...], acc_scr[...])

    @pl.when(j == pl.num_programs(1) - 1)
    def _():
        n_new = knew_ref.shape[0]
        r_i = lax.broadcasted_iota(jnp.int32, (rows, n_new), 0)
        c_i = lax.broadcasted_iota(jnp.int32, (rows, n_new), 1)
        ok = ((lax.shift_right_logical(r_i, int(math.log2(2 * tq))) == (c_i & (H_A - 1)))
              & (lax.shift_right_logical(c_i, int(math.log2(H_A))) <= (r_i & (tq - 1))))
        s = jnp.where(ok, _dot_nt(qq, knew_ref[...].astype(BF16)), NEG)
        _, l, acc = _online_update(s, vnew_ref[...].astype(BF16), m_scr[...], l_scr[...], acc_scr[...])
        o = acc / l
        lam = _lambda(lamp_ref, lam_init)
        for h in range(H_A):
            base = h * 2 * tq
            d = o[base:base + tq] - lam * o[base + tq:base + 2 * tq]
            o_ref[:, h * 2 * DH_A:(h + 1) * 2 * DH_A] = _rms(d, g_ref[...]) * (1.0 - lam_init)


def _diff_decode(page_table, lam_p, subln_g, q, k_new, v_new, cache_k, cache_v, lam_init):
    bs, tq, w = q.shape
    n_pages_total = page_table.shape[1]
    pc = PAGES_PER_STEP
    rows = H_A * 2 * tq
    dv = 2 * DH_A
    page_rows = cache_k.shape[1]
    per_b = lambda b, j, pt: (b, 0, 0)
    const = lambda b, j, pt: (0, 0)
    grid_spec = pltpu.PrefetchScalarGridSpec(
        num_scalar_prefetch=1,
        grid=(bs, n_pages_total // pc),
        in_specs=[pl.BlockSpec(lam_p.shape, const), pl.BlockSpec((1, dv), const),
                  pl.BlockSpec((None, tq, w), per_b),
                  pl.BlockSpec((None,) + k_new.shape[1:], per_b),
                  pl.BlockSpec((None,) + v_new.shape[1:], per_b)]
        + _page_specs(page_rows, pc) + _page_specs(page_rows, pc),
        out_specs=pl.BlockSpec((None, tq, w), per_b),
        scratch_shapes=[pltpu.VMEM((rows, dv), BF16), pltpu.VMEM((rows, page_rows), F32),
                        pltpu.VMEM((rows, 1), F32), pltpu.VMEM((rows, 1), F32), pltpu.VMEM((rows, dv), F32)])
    return pl.pallas_call(
        functools.partial(_diff_decode_body, n_pages=pc, tq=tq, lam_init=lam_init),
        grid_spec=grid_spec,
        out_shape=jax.ShapeDtypeStruct((bs, tq, w), F32),
        compiler_params=_cparams(("parallel", "arbitrary"), 58),
        name="diff_decode",
    )(page_table, lam_p, subln_g, q, k_new, v_new, *([cache_k] * pc), *([cache_v] * pc))


def _chunk_perm(page_tokens):
    n = page_tokens * G_B
    chunks = page_tokens // CMP_STRIDE
    out = np.arange(n)
    l, g, c = out // (G_B * chunks), (out // chunks) % G_B, out % chunks
    src = (c * CMP_STRIDE + l) * G_B + g
    perm = np.zeros((n, n), np.float32)
    perm[out, src] = 1.0
    return jnp.asarray(perm, BF16)


def _compress_decode_body(pt_ref, perm_ref, wk_ref, wv_ref, c_ref, *rest, n_pages):
    k_pages = rest[:n_pages]
    v_pages = rest[n_pages:2 * n_pages]
    yk_ref, yv_ref, a_scr = rest[2 * n_pages:]
    chunks = k_pages[0].shape[0] // (G_B * CMP_STRIDE)
    per_g = n_pages * chunks
    for a, (pages, w_ref, y_ref) in enumerate(((k_pages, wk_ref, yk_ref), (v_pages, wv_ref, yv_ref))):
        side_by_side = jnp.concatenate([pages[p][...].astype(BF16) for p in range(n_pages)], axis=1)
        pp = _dot(perm_ref[...], side_by_side)
        for l in range(CMP_STRIDE):
            for g in range(G_B):
                src = (l * G_B + g) * chunks
                for p in range(0, n_pages, 2):
                    dst = g * per_g + p * chunks
                    pair = jnp.concatenate([pp[src:src + chunks, p * DH_B:(p + 1) * DH_B],
                                            pp[src:src + chunks, (p + 1) * DH_B:(p + 2) * DH_B]], axis=0)
                    a_scr[dst:dst + 2 * chunks, l * DH_B:(l + 1) * DH_B] = pair.astype(BF16)
        y = _dot(a_scr[...], w_ref[...])
        y = jnp.concatenate([y[:, :DH_B] + c_ref[a, 0:1, :], y[:, DH_B:]], axis=1)
        for g in range(G_B):
            y_ref[g] = y[g * per_g:(g + 1) * per_g]


def _compress_decode(page_table, perm, wk, wv, const, cache_k, cache_v):
    bs, n_pages_total = page_table.shape
    pc = GROUP_PAGES_PER_STEP
    page_rows = cache_k.shape[1]
    chunks = page_rows // (G_B * CMP_STRIDE)
    n_chunk = n_pages_total * chunks
    const2 = lambda b, j, pt: (0, 0)
    y_spec = pl.BlockSpec((None, G_B, pc * chunks, 2 * DH_B), lambda b, j, pt: (b, 0, j, 0))
    y_shape = jax.ShapeDtypeStruct((bs, G_B, n_chunk, 2 * DH_B), F32)
    grid_spec = pltpu.PrefetchScalarGridSpec(
        num_scalar_prefetch=1,
        grid=(bs, n_pages_total // pc),
        in_specs=[pl.BlockSpec(perm.shape, const2), pl.BlockSpec(wk.shape, const2), pl.BlockSpec(wv.shape, const2),
                  pl.BlockSpec(const.shape, lambda b, j, pt: (0, 0, 0))]
        + _page_specs(page_rows, pc) + _page_specs(page_rows, pc),
        out_specs=[y_spec, y_spec],
        scratch_shapes=[pltpu.VMEM((G_B * pc * chunks, CMP_STRIDE * DH_B), BF16)])
    return pl.pallas_call(
        functools.partial(_compress_decode_body, n_pages=pc),
        grid_spec=grid_spec,
        out_shape=[y_shape, y_shape],
        compiler_params=_cparams(("parallel", "arbitrary"), 40),
        name="compress_decode",
    )(page_table, perm, wk, wv, const, *([cache_k] * pc), *([cache_v] * pc))


def _nsa_decode_body(pt_ref, q_ref, yk_ref, yv_ref, selmap_ref, expand_ref, gate_ref, ksn_ref, vsn_ref, kwn_ref,
                     vwn_ref, wk_ref, wv_ref, *rest, n_pages, tq, past_len, n_sel, n_top):
    k_pages = rest[:n_pages]
    v_pages = rest[n_pages:2 * n_pages]
    o_ref, wko_ref, wvo_ref, q_scr, sel_scr, bias_scr, oc_scr, m_scr, l_scr, acc_scr = rest[2 * n_pages:]
    j = pl.program_id(1)
    rows = H_B * tq
    rows_g = HPG_B * tq
    n_keys = k_pages[0].shape[0]
    log_tq = int(math.log2(tq))
    log_g = int(math.log2(G_B))

    def group_of_row(shape):
        return lax.shift_right_logical(lax.broadcasted_iota(jnp.int32, shape, 0), int(math.log2(rows_g)))

    def qpos_of_row(shape):
        return past_len + (lax.broadcasted_iota(jnp.int32, shape, 0) & (tq - 1))

    @pl.when(j == 0)
    def _():
        q = q_ref[...]
        q_scr[...] = jnp.concatenate([q[:, h * DH_B:(h + 1) * DH_B] for h in range(H_B)], axis=0).astype(BF16)
        qpos_col = past_len + lax.broadcasted_iota(jnp.int32, (tq, 1), 0)
        for g in range(G_B):
            qg = q_scr[g * rows_g:(g + 1) * rows_g, :]
            kcmp = _compressed_kv(yk_ref[g])
            vcmp = _compressed_kv(yv_ref[g])
            n_cmp = kcmp.shape[0]
            s = _dot_nt(qg, kcmp)
            cmp_end = lax.broadcasted_iota(jnp.int32, (rows_g, n_cmp), 1) * CMP_STRIDE + (CMP_BLOCK - 1)
            p = _masked_softmax(s, cmp_end <= qpos_of_row((rows_g, n_cmp)))
            oc_scr[g * rows_g:(g + 1) * rows_g, :] = _dot(p.astype(BF16), vcmp)
            p_sum = p[0:tq]
            for h in range(1, HPG_B):
                p_sum = p_sum + p[h * tq:(h + 1) * tq]
            imp = jnp.dot(p_sum, selmap_ref[...], precision=lax.Precision.HIGHEST, preferred_element_type=F32)
            sel = _select_blocks(imp, qpos_col, n_sel, n_top)
            for h in range(HPG_B):
                sel_scr[g * rows_g + h * tq:g * rows_g + (h + 1) * tq, :] = sel
        sel_rows = sel_scr[...]
        n_lanes = sel_rows.shape[1]
        exp_slots = expand_ref.shape[1]
        blocks_per_exp = exp_slots // (SEL_BLOCK * G_B)
        slot = lax.broadcasted_iota(jnp.int32, (rows, exp_slots), 1)
        same_group = group_of_row((rows, exp_slots)) == (slot & (G_B - 1))
        for c in range(bias_scr.shape[1] // exp_slots):
            shifted = pltpu.roll(sel_rows, (n_lanes - c * blocks_per_exp) % n_lanes, 1) if c else sel_rows
            chosen = _dot(shifted[:, :LANES].astype(BF16), expand_ref[...])
            bias_scr[:, c * exp_slots:(c + 1) * exp_slots] = jnp.where((chosen > 0.5) & same_group, 0.0, NEG)
        m_scr[...] = jnp.full(m_scr.shape, NEG, F32)
        l_scr[...] = jnp.zeros(l_scr.shape, F32)
        acc_scr[...] = jnp.zeros(acc_scr.shape, F32)

    qq = q_scr[...]
    step_slots = n_pages * n_keys
    bias = bias_scr[:, pl.ds(pl.multiple_of(j * step_slots, step_slots), step_slots)]
    ss = [_dot_nt(qq, k_pages[r][...].astype(BF16)) + bias[:, r * n_keys:(r + 1) * n_keys] for r in range(n_pages)]
    vs = [v_pages[r][...].astype(BF16) for r in range(n_pages)]
    m_scr[...], l_scr[...], acc_scr[...] = _online_update_pages(ss, vs, m_scr[...], l_scr[...], acc_scr[...])

    @pl.when(j == pl.num_programs(1) - 1)
    def _():
        sel_rows = sel_scr[...]
        sel_lane = lax.broadcasted_iota(jnp.int32, sel_rows.shape, 1)

        def block_selected(blk):
            return jnp.sum(jnp.where(sel_lane == blk, sel_rows, 0.0), axis=1, keepdims=True) > 0.5

        n_new = ksn_ref.shape[0]
        c_new = lax.broadcasted_iota(jnp.int32, (rows, n_new), 1)
        t_new = lax.shift_right_logical(c_new, log_g)
        grp_ok = group_of_row((rows, n_new)) == (c_new & (G_B - 1))
        causal = (past_len + t_new) <= qpos_of_row((rows, n_new))
        ok = grp_ok & causal & block_selected(past_len // SEL_BLOCK)
        s = jnp.where(ok, _dot_nt(qq, ksn_ref[...].astype(BF16)), NEG)
        _, l, acc = _online_update(s, vsn_ref[...].astype(BF16), m_scr[...], l_scr[...], acc_scr[...])
        o_s = acc / l

        n_win = wk_ref.shape[0]
        win_tokens = n_win // G_B
        c_w = lax.broadcasted_iota(jnp.int32, (rows, n_win), 1)
        kpos = past_len - win_tokens + lax.shift_right_logical(c_w, log_g)
        dist = qpos_of_row((rows, n_win)) - kpos
        ok_w = (group_of_row((rows, n_win)) == (c_w & (G_B - 1))) & (dist >= 0) & (dist <= WINDOW) & (kpos >= 0)
        s_w = jnp.where(ok_w, _dot_nt(qq, wk_ref[...].astype(BF16)), NEG)
        dist_n = qpos_of_row((rows, n_new)) - (past_len + t_new)
        ok_n = grp_ok & (dist_n >= 0) & (dist_n <= WINDOW)
        s_n = jnp.where(ok_n, _dot_nt(qq, kwn_ref[...].astype(BF16)), NEG)
        m = jnp.maximum(jnp.max(s_w, axis=-1, keepdims=True), jnp.max(s_n, axis=-1, keepdims=True))
        e_w = jnp.where(ok_w, jnp.exp2(s_w - m), 0.0)
        e_n = jnp.where(ok_n, jnp.exp2(s_n - m), 0.0)
        l_w = jnp.sum(e_w, axis=-1, keepdims=True) + jnp.sum(e_n, axis=-1, keepdims=True)
        o_w = (_dot(e_w.astype(BF16), wv_ref[...].astype(BF16))
               + _dot(e_n.astype(BF16), vwn_ref[...].astype(BF16))) / jnp.where(l_w > 0.0, l_w, 1.0)

        gates = gate_ref[...]
        o_c = oc_scr[...]
        for head in range(H_B):
            rs = slice(head * tq, (head + 1) * tq)
            o_ref[:, head * DH_B:(head + 1) * DH_B] = (
                gates[:, 3 * head:3 * head + 1] * o_c[rs] + gates[:, 3 * head + 1:3 * head + 2] * o_s[rs]
                + gates[:, 3 * head + 2:3 * head + 3] * o_w[rs])

        keep = n_win - n_new
        wko_ref[0:keep, :] = wk_ref[n_new:n_win, :]
        wko_ref[keep:n_win, :] = kwn_ref[...]
        wvo_ref[0:keep, :] = wv_ref[n_new:n_win, :]
        wvo_ref[keep:n_win, :] = vwn_ref[...]


def _nsa_decode(page_table, q, yk, yv, selmap, gates, ks_new, vs_new, kw_new, vw_new, win_k, win_v,
                cache_k, cache_v, past_len, n_sel):
    bs, tq, w = q.shape
    n_pages_total = page_table.shape[1]
    pc = GROUP_PAGES_PER_STEP
    rows = H_B * tq
    page_rows = cache_k.shape[1]
    exp_slots = 2048
    slots_per_block = SEL_BLOCK * G_B
    expand = jnp.asarray(np.arange(LANES)[:, None] == (np.arange(exp_slots)[None, :] // slots_per_block), BF16)
    assert (n_pages_total * page_rows) % exp_slots == 0 and exp_slots // slots_per_block <= LANES
    per_b3 = lambda b, j, pt: (b, 0, 0)
    per_b4 = lambda b, j, pt: (b, 0, 0, 0)
    const2 = lambda b, j, pt: (0, 0)
    blk3 = lambda a: pl.BlockSpec((None,) + a.shape[1:], per_b3)
    y_spec = pl.BlockSpec((None,) + yk.shape[1:], per_b4)
    grid_spec = pltpu.PrefetchScalarGridSpec(
        num_scalar_prefetch=1,
        grid=(bs, n_pages_total // pc),
        in_specs=[blk3(q), y_spec, y_spec, pl.BlockSpec(selmap.shape, const2), pl.BlockSpec(expand.shape, const2),
                  blk3(gates), blk3(ks_new), blk3(vs_new), blk3(kw_new), blk3(vw_new), blk3(win_k), blk3(win_v)]
        + _page_specs(page_rows, pc) + _page_specs(page_rows, pc),
        out_specs=[blk3(q), blk3(win_k), blk3(win_v)],
        scratch_shapes=[pltpu.VMEM((rows, DH_B), BF16), pltpu.VMEM((rows, selmap.shape[1]), F32),
                        pltpu.VMEM((rows, n_pages_total * page_rows), F32),
                        pltpu.VMEM((rows, DH_B), F32), pltpu.VMEM((rows, 1), F32), pltpu.VMEM((rows, 1), F32),
                        pltpu.VMEM((rows, DH_B), F32)])
    return pl.pallas_call(
        functools.partial(_nsa_decode_body, n_pages=pc, tq=tq, past_len=past_len, n_sel=n_sel,
                          n_top=min(SEL_TOP_N, n_sel)),
        grid_spec=grid_spec,
        out_shape=[jax.ShapeDtypeStruct((bs, tq, w), F32), jax.ShapeDtypeStruct(win_k.shape, F32),
                   jax.ShapeDtypeStruct(win_v.shape, F32)],
        compiler_params=_cparams(("parallel", "arbitrary"), 48),
        name="nsa_decode",
    )(page_table, q, yk, yv, selmap, expand, gates, ks_new, vs_new, kw_new, vw_new, win_k, win_v,
      *([cache_k] * pc), *([cache_v] * pc))


def kernel(x_prompt, x_sample, cache_diff_k, cache_diff_v, cache_cmp_k, cache_cmp_v, cache_sel_k, cache_sel_v,
           cache_win_k, cache_win_v, page_table, c_prompt, c_sample, w_mod, b_mod, norm_g, w_in, w_out, lam_p,
           subln_g, cmp_pe, cmp_phi, ffn_gate, ffn_up, ffn_down):
    depth = w_mod.shape[0]
    assert depth == 1, "single-layer step"
    B, T, D = x_prompt.shape
    Bs, Ts, _ = x_sample.shape
    n_pool, page = cache_diff_k.shape[1], cache_diff_k.shape[2]
    past_len = page_table.shape[1] * page
    win_len = cache_win_k.shape[2]
    assert T % 512 == 0 and win_len <= T and win_len == WINDOW and past_len >= win_len
    assert past_len % SEL_BLOCK == 0 and Ts <= SEL_BLOCK and (Ts & (Ts - 1)) == 0
    assert (past_len + Ts - CMP_BLOCK) // CMP_STRIDE + 1 <= past_len // CMP_STRIDE
    lam_init = 0.8 - 0.6 * math.exp(-0.3 * 0)
    Np, Ns = B * T, Bs * Ts
    gq = G_B * DH_B

    wg, wu, wd = ffn_gate[0].astype(BF16), ffn_up[0].astype(BF16), ffn_down[0].astype(BF16)
    w_pad = jnp.pad(w_in[0].astype(BF16), ((0, 0), (0, N_IN_BLOCKS * IN_BLOCK - w_in.shape[2])))
    w_out_a, w_out_b = w_out[0, :H_A * 2 * DH_A].astype(BF16), w_out[0, H_A * 2 * DH_A:].astype(BF16)
    ng = norm_g[0].reshape(6, 1, D)
    wck, pek = _compress_weights(cmp_pe[0, 0], cmp_phi[0, 0])
    wcv, pev = _compress_weights(cmp_pe[0, 1], cmp_phi[0, 1])
    cmp_const = _cmp_const(pek, pev, wck, wcv)

    n_c = B + Bs
    c_all = jnp.pad(jnp.concatenate([c_prompt, c_sample], axis=0), ((0, -n_c % 8), (0, 0)))
    mod = _adaln_mod(c_all, w_mod[0], b_mod[0][None]).reshape(c_all.shape[0], 9, D)
    mod_p = mod[:B].transpose(1, 0, 2).reshape(9, B, 1, D)
    mod_s = jnp.repeat(mod[B:n_c].transpose(1, 0, 2), Ts, axis=1)

    def split_states(pr, batch, seq):
        return (pr["k_a"].reshape(1, batch, seq, H_A, 2 * DH_A), pr["v_a"].reshape(1, batch, seq, H_A, 2 * DH_A),
                pr["kc"].reshape(1, batch, seq, G_B, DH_B), pr["vc"].reshape(1, batch, seq, G_B, DH_B),
                pr["ks"].reshape(1, batch, seq, G_B, DH_B), pr["vs"].reshape(1, batch, seq, G_B, DH_B))

    names = ("q_a", "k_a", "k_a_bf", "v_a", "v_a_bf", "q_b", "kc", "vc", "ks", "vs", "kw", "vw", "nkv", "gates")

    tm_p = 512
    xp = x_prompt.reshape(Np, D)
    x1 = _ffn(xp, mod_p, (0, 1, 2), T, ng[0], ng[1], wg, wu, wd, 0, tm_p, FFN_TF)
    pr = dict(zip(names, _inproj(x1, mod_p, 3, 4, T, ng[2], w_pad, _rope_tables(np.arange(T)), tm_p)))
    o_a = _diff_prompt(pr["q_a"], pr["k_a_bf"], pr["v_a_bf"], lam_p[0], subln_g[0][None], B, T, lam_init)
    yk, yv = _compress_prompt(pr["kc"], pr["vc"], wck, wcv, cmp_const, B, T)
    n_cmp_p = (T - CMP_BLOCK) // CMP_STRIDE + 1
    n_sel_p = -(-T // SEL_BLOCK)
    assert n_sel_p <= LANES
    selmap_p = _sel_map(T // CMP_STRIDE, n_cmp_p, n_sel_p, n_sel_p).T
    o_b = _nsa_prompt(pr["q_b"], yk, yv, pr["nkv"], pr["gates"], selmap_p, B, T)
    x2 = _outproj(o_a, o_b, w_out_a, w_out_b, x1, mod_p, 5, T, ng[3], tm_p)
    y_p = _ffn(x2, mod_p, (6, 7, 8), T, ng[4], ng[5], wg, wu, wd, 1, tm_p, FFN_TF).reshape(B, T, D)
    st_p = split_states(pr, B, T)
    wk_p = pr["kw"].reshape(1, B, T, G_B, DH_B)[:, :, T - win_len:]
    wv_p = pr["vw"].reshape(1, B, T, G_B, DH_B)[:, :, T - win_len:]

    xs = x_sample.reshape(Ns, D)
    x1s = _ffn(xs, mod_s, (0, 1, 2), 1, ng[0], ng[1], wg, wu, wd, 0, Ns, FFN_TF)
    pos_s = np.tile(past_len + np.arange(Ts), Bs)
    ps = dict(zip(names, _inproj(x1s, mod_s, 3, 4, 1, ng[2], w_pad, _rope_tables(pos_s), Ns)))
    ck = cache_diff_k[0].reshape(n_pool, page * H_A, 2 * DH_A)
    cv = cache_diff_v[0].reshape(n_pool, page * H_A, 2 * DH_A)
    o_as = _diff_decode(page_table, lam_p[0], subln_g[0][None],
                        ps["q_a"].astype(F32).reshape(Bs, Ts, H_A * 2 * DH_A),
                        ps["k_a"].reshape(Bs, Ts * H_A, 2 * DH_A), ps["v_a"].reshape(Bs, Ts * H_A, 2 * DH_A),
                        ck, cv, lam_init)
    pool3 = lambda c: c[0].reshape(n_pool, page * G_B, DH_B)
    yks, yvs = _compress_decode(page_table, _chunk_perm(page), wck, wcv, cmp_const,
                                pool3(cache_cmp_k), pool3(cache_cmp_v))
    tk_s = past_len + Ts
    n_cmp_s = (tk_s - CMP_BLOCK) // CMP_STRIDE + 1
    n_sel_s = -(-tk_s // SEL_BLOCK)
    selmap_s = _sel_map(past_len // CMP_STRIDE, n_cmp_s, n_sel_s, -(-n_sel_s // LANES) * LANES)
    new3 = lambda a: a.reshape(Bs, Ts * G_B, DH_B)
    o_bs, wk_s, wv_s = _nsa_decode(
        page_table, ps["q_b"].astype(F32).reshape(Bs, Ts, H_B * DH_B), yks, yvs, selmap_s,
        ps["gates"].reshape(Bs, Ts, LANES), new3(ps["ks"]), new3(ps["vs"]), new3(ps["kw"]), new3(ps["vw"]),
        cache_win_k[0].reshape(Bs, win_len * G_B, DH_B), cache_win_v[0].reshape(Bs, win_len * G_B, DH_B),
        pool3(cache_sel_k), pool3(cache_sel_v), past_len, n_sel_s)
    x2s = _outproj(o_as.reshape(Ns, -1).astype(BF16), o_bs.reshape(Ns, -1).astype(BF16), w_out_a, w_out_b,
                   x1s, mod_s, 5, 1, ng[3], Ns)
    y_s = _ffn(x2s, mod_s, (6, 7, 8), 1, ng[4], ng[5], wg, wu, wd, 1, Ns, FFN_TF).reshape(Bs, Ts, D)
    st_s = split_states(ps, Bs, Ts)
    wk_s = wk_s.reshape(1, Bs, win_len, G_B, DH_B)
    wv_s = wv_s.reshape(1, Bs, win_len, G_B, DH_B)

    return (y_p, y_s) + st_p + (wk_p, wv_p) + st_s + (wk_s, wv_s)
```

```python
import functools
import math

import numpy as np
import jax
import jax.numpy as jnp
from jax import lax
from jax.experimental import pallas as pl
from jax.experimental.pallas import tpu as pltpu

F32 = jnp.float32
BF16 = jnp.bfloat16

DH_A = 64
H_A = 8
DH_B = 128
H_B = 8
G_B = 2
HPG_B = H_B // G_B
CMP_BLOCK = 32
CMP_STRIDE = 16
SEL_BLOCK = 64
SEL_TOP_N = 16
N_LOCAL = 2
WINDOW = 512
ROPE_THETA = 10000.0
EPS = 1e-6
NEG = -1e30
FORCE = 1e4
LOG2E = math.log2(math.e)

LANES = 128
MIB = 1024 * 1024
PAGES_PER_STEP = 16
GROUP_PAGES_PER_STEP = 32
FFN_TF = 1024


def _dot(a, b):
    return jnp.dot(a, b, preferred_element_type=F32)


def _dot_nt(a, b):
    return lax.dot_general(a, b, (((1,), (1,)), ((), ())), preferred_element_type=F32)


def _rms(x, g):
    return x * lax.rsqrt(jnp.mean(x * x, axis=-1, keepdims=True) + EPS) * g


def _cparams(sem, vmem_mib):
    return pltpu.CompilerParams(dimension_semantics=sem, vmem_limit_bytes=vmem_mib * MIB)


def _mod_body(c_ref, w_ref, b_ref, o_ref):
    c = c_ref[...]
    a = (c * jax.nn.sigmoid(c)).astype(BF16)
    o_ref[...] = _dot(a, w_ref[...].astype(BF16)) + b_ref[...]


def _adaln_mod(c, w_mod, b_mod):
    R, D = c.shape
    n_out = w_mod.shape[1]
    tn = 1024
    return pl.pallas_call(
        _mod_body,
        grid=(n_out // tn,),
        in_specs=[pl.BlockSpec((R, D), lambda j: (0, 0)),
                  pl.BlockSpec((D, tn), lambda j: (0, j)),
                  pl.BlockSpec((1, tn), lambda j: (0, j))],
        out_specs=pl.BlockSpec((R, tn), lambda j: (0, j)),
        out_shape=jax.ShapeDtypeStruct((R, n_out), F32),
        compiler_params=_cparams(("arbitrary",), 40),
        name="adaln_mod",
    )(c, w_mod, b_mod)


def _mod_spec(mod, k, tm, rows_per_group):
    if mod.ndim == 4:
        return pl.BlockSpec((None, None, 1, mod.shape[-1]),
                            lambda i, *_, k=k: (k, (i * tm) // rows_per_group, 0, 0))
    return pl.BlockSpec((None, tm, mod.shape[-1]), lambda i, *_, k=k: (k, i, 0))


def _ffn_body(x_ref, sh_ref, sc_ref, gt_ref, gpre_ref, gpost_ref, wg_ref, wu_ref, wd_ref, o_ref,
              h_scr, acc_scr, *, n_chunk):
    f = pl.program_id(1)

    @pl.when(f == 0)
    def _():
        h = _rms(x_ref[...], gpre_ref[...]) * (1.0 + sc_ref[...]) + sh_ref[...]
        h_scr[...] = h.astype(BF16)
        acc_scr[...] = jnp.zeros(acc_scr.shape, F32)

    h = h_scr[...]
    tf = wg_ref.shape[1]
    hw = min(tf, 512)
    parts = []
    for c in range(tf // hw):
        g = _dot(h, wg_ref[:, c * hw:(c + 1) * hw])
        u = _dot(h, wu_ref[:, c * hw:(c + 1) * hw])
        parts.append((g * jax.nn.sigmoid(g) * u).astype(BF16))
    a = parts[0] if len(parts) == 1 else jnp.concatenate(parts, axis=1)
    d = acc_scr.shape[1]
    cw = d // n_chunk
    for n in range(n_chunk):
        acc_scr[:, n * cw:(n + 1) * cw] += _dot(a, wd_ref[:, n * cw:(n + 1) * cw])

    @pl.when(f == pl.num_programs(1) - 1)
    def _():
        o_ref[...] = x_ref[...] + 0.5 * gt_ref[...] * _rms(acc_scr[...], gpost_ref[...])


def _ffn(x, mod, ks, rows_per_group, g_pre, g_post, wg, wu, wd, layer, tm, tf):
    n, d = x.shape
    dff = wg.shape[2]
    row = lambda i, f: (i, 0)
    const = lambda i, f: (0, 0)
    return pl.pallas_call(
        functools.partial(_ffn_body, n_chunk=4),
        grid=(n // tm, dff // tf),
        in_specs=[pl.BlockSpec((tm, d), row),
                  _mod_spec(mod, ks[0], tm, rows_per_group),
                  _mod_spec(mod, ks[1], tm, rows_per_group),
                  _mod_spec(mod, ks[2], tm, rows_per_group),
                  pl.BlockSpec((1, d), const), pl.BlockSpec((1, d), const),
                  pl.BlockSpec((None, d, tf), lambda i, f: (layer, 0, f)),
                  pl.BlockSpec((None, d, tf), lambda i, f: (layer, 0, f)),
                  pl.BlockSpec((None, tf, d), lambda i, f: (layer, f, 0))],
        out_specs=pl.BlockSpec((tm, d), row),
        out_shape=jax.ShapeDtypeStruct((n, d), F32),
        scratch_shapes=[pltpu.VMEM((tm, d), BF16), pltpu.VMEM((tm, d), F32)],
        compiler_params=_cparams(("parallel", "arbitrary"), 60),
        name="ffn",
    )(x, mod, mod, mod, g_pre, g_post, wg, wu, wd)


IN_UNIT = 1024
N_IN_UNITS = 6
IN_BLOCK = 2048
N_IN_BLOCKS = N_IN_UNITS * IN_UNIT // IN_BLOCK


def _inproj_body(x_ref, sh_ref, sc_ref, g_ref, w_ref, ca_ref, sa_ref, cb_ref, sb_ref,
                 qa_ref, ka_ref, kab_ref, va_ref, vab_ref, qb_ref, kc_ref, vc_ref, ks_ref, vs_ref,
                 kw_ref, vw_ref, nkv_ref, gate_ref, h_scr):
    j = pl.program_id(1)

    @pl.when(j == 0)
    def _():
        h = _rms(x_ref[...], g_ref[...]) * (1.0 + sc_ref[...]) + sh_ref[...]
        h_scr[...] = h.astype(BF16)

    lane = lax.broadcasted_iota(jnp.int32, (1, LANES), 1)
    low32 = (lane & (DH_A - 1)) < (DH_A // 2)

    def cols(c):
        y = _dot(h_scr[...], w_ref[:, c * 256:(c + 1) * 256])
        return y[:, :LANES], y[:, LANES:]

    def rope64(y):
        partner = jnp.where(low32, pltpu.roll(y, LANES - DH_A // 2, 1), pltpu.roll(y, DH_A // 2, 1))
        return y * ca_ref[...] + partner * sa_ref[...]

    def rope128(y):
        return y * cb_ref[...] + pltpu.roll(y, DH_B // 2, 1) * sb_ref[...]

    def tile(t):
        return slice(t * LANES, (t + 1) * LANES)

    tm = x_ref.shape[0]

    def group_rows(g):
        return pl.ds(g, tm, stride=G_B)

    def unit_q_a(base):
        for c in range(4):
            for t, y in enumerate(cols(base + c)):
                qa_ref[:, tile(2 * c + t)] = (rope64(y) * (DH_A ** -0.5 * LOG2E)).astype(BF16)

    def unit_k_a(base):
        for c in range(4):
            for t, y in enumerate(cols(base + c)):
                r = rope64(y)
                ka_ref[:, tile(2 * c + t)] = r
                kab_ref[:, tile(2 * c + t)] = r.astype(BF16)

    def unit_v_a(base):
        for c in range(4):
            for t, y in enumerate(cols(base + c)):
                va_ref[:, tile(2 * c + t)] = y
                vab_ref[:, tile(2 * c + t)] = y.astype(BF16)

    def unit_q_b(base):
        for c in range(4):
            for t, y in enumerate(cols(base + c)):
                qb_ref[:, tile(2 * c + t)] = (rope128(y) * (DH_B ** -0.5 * LOG2E)).astype(BF16)

    def unit_cmp_sel(base):
        for c, (ref, roped) in enumerate(((kc_ref, True), (vc_ref, False), (ks_ref, True), (vs_ref, False))):
            for t, y in enumerate(cols(base + c)):
                r = rope128(y) if roped else y
                ref[group_rows(t), :] = r
                if c >= 2:
                    nkv_ref[:, tile(2 * (c - 2) + t)] = r.astype(BF16)

    def unit_win_gate(base):
        for c, (ref, roped) in enumerate(((kw_ref, True), (vw_ref, False))):
            for t, y in enumerate(cols(base + c)):
                r = rope128(y) if roped else y
                ref[group_rows(t), :] = r
                nkv_ref[:, tile(4 + 2 * c + t)] = r.astype(BF16)
        gate_ref[...] = jax.nn.sigmoid(cols(base + 2)[0])

    units = (unit_q_a, unit_k_a, unit_v_a, unit_q_b, unit_cmp_sel, unit_win_gate)
    per_step = IN_BLOCK // IN_UNIT
    for step in range(N_IN_UNITS // per_step):
        @pl.when(j == step)
        def _(step=step):
            for u in range(per_step):
                units[step * per_step + u](u * (IN_UNIT // 256))


def _inproj(x, mod, k_shift, k_scale, rows_per_group, g, w_pad, tabs, tm):
    n, d = x.shape
    gq = G_B * DH_B
    n_tab = tabs[0].shape[0] // tm
    row = lambda i, j: (i, 0)
    const = lambda i, j: (0, 0)
    tab = pl.BlockSpec((tm, LANES), lambda i, j: (i % n_tab, 0))
    wide = lambda dt: jax.ShapeDtypeStruct((n, IN_UNIT), dt)
    grouped = jax.ShapeDtypeStruct((n * G_B, DH_B), F32)
    out_shape = [wide(BF16), wide(F32), wide(BF16), wide(F32), wide(BF16), wide(BF16)] + [grouped] * 6 + [
        wide(BF16), jax.ShapeDtypeStruct((n, LANES), F32)]
    out_specs = [pl.BlockSpec((tm * s.shape[0] // n, s.shape[1]), row) for s in out_shape]
    return pl.pallas_call(
        _inproj_body,
        grid=(n // tm, N_IN_BLOCKS),
        in_specs=[pl.BlockSpec((tm, d), row),
                  _mod_spec(mod, k_shift, tm, rows_per_group),
                  _mod_spec(mod, k_scale, tm, rows_per_group),
                  pl.BlockSpec((1, d), const),
                  pl.BlockSpec((d, IN_BLOCK), lambda i, j: (0, j)),
                  tab, tab, tab, tab],
        out_specs=out_specs,
        out_shape=out_shape,
        scratch_shapes=[pltpu.VMEM((tm, d), BF16)],
        compiler_params=_cparams(("parallel", "arbitrary"), 58),
        name="inproj",
    )(x, mod, mod, g, w_pad, *tabs)


def _rope_tables(pos):
    pos = np.asarray(pos, np.float64)[:, None]

    def tables(head_dim):
        half = head_dim // 2
        inv = ROPE_THETA ** (-np.arange(half, dtype=np.float64) / half)
        ang = pos * inv[None, :]
        cos = np.concatenate([np.cos(ang), np.cos(ang)], axis=1)
        sin = np.concatenate([-np.sin(ang), np.sin(ang)], axis=1)
        rep = LANES // head_dim
        return (jnp.asarray(np.tile(cos, (1, rep)), F32), jnp.asarray(np.tile(sin, (1, rep)), F32))

    ca, sa = tables(DH_A)
    cb, sb = tables(DH_B)
    return ca, sa, cb, sb


def _lambda(lamp_ref, lam_init):
    lp = lamp_ref[...]
    a = jnp.sum(lp[0:1] * lp[1:2], axis=1, keepdims=True)
    b = jnp.sum(lp[2:3] * lp[3:4], axis=1, keepdims=True)
    return jnp.exp(a) - jnp.exp(b) + lam_init


def _online_update(s, v, m, l, acc):
    m_new = jnp.maximum(m, jnp.max(s, axis=-1, keepdims=True))
    p = jnp.exp2(s - m_new)
    alpha = jnp.exp2(m - m_new)
    l = alpha * l + jnp.sum(p, axis=-1, keepdims=True)
    acc = alpha * acc + _dot(p.astype(BF16), v)
    return m_new, l, acc


def _online_update_pages(ss, vs, m, l, acc):
    m_blk = jnp.max(ss[0], axis=-1, keepdims=True)
    for s in ss[1:]:
        m_blk = jnp.maximum(m_blk, jnp.max(s, axis=-1, keepdims=True))
    m_new = jnp.maximum(m, m_blk)
    alpha = jnp.exp2(m - m_new)
    l = alpha * l
    acc = alpha * acc
    for s, v in zip(ss, vs):
        p = jnp.exp2(s - m_new)
        l = l + jnp.sum(p, axis=-1, keepdims=True)
        acc = acc + _dot(p.astype(BF16), v)
    return m_new, l, acc


def _softmax_init(rows, dv):
    return (jnp.full((rows, 1), NEG, F32), jnp.zeros((rows, 1), F32), jnp.zeros((rows, dv), F32))


def _masked_softmax(s, mask):
    sm = jnp.where(mask, s, NEG)
    m = jnp.max(sm, axis=-1, keepdims=True)
    e = jnp.where(mask, jnp.exp2(sm - m), 0.0)
    l = jnp.sum(e, axis=-1, keepdims=True)
    return e / jnp.where(l > 0.0, l, 1.0)


def _select_blocks(imp, qpos, n_blocks, n_top):
    shape = imp.shape
    blk = lax.broadcasted_iota(jnp.int32, shape, 1)
    rel = lax.shift_right_logical(qpos, int(math.log2(SEL_BLOCK))) - blk
    visible = rel >= 0
    forced = (blk == 0) | (visible & (rel < N_LOCAL))
    score = jnp.where(visible, jnp.where(forced, FORCE + imp, imp), NEG)
    rank = jnp.zeros(shape, F32)
    for i in range(n_blocks):
        col = score[:, i:i + 1]
        beats = (col > score) | ((col == score) & (blk > i))
        rank = rank + jnp.where(beats, 1.0, 0.0)
    return jnp.where((rank < n_top) & visible, 1.0, 0.0)


def _select_blocks_t(imp_t, qpos_row, n_blocks, n_top):
    shape = imp_t.shape
    blk = lax.broadcasted_iota(jnp.int32, shape, 0)
    rel = lax.shift_right_logical(qpos_row, int(math.log2(SEL_BLOCK))) - blk
    visible = rel >= 0
    forced = (blk == 0) | (visible & (rel < N_LOCAL))
    score = jnp.where(visible, jnp.where(forced, FORCE + imp_t, imp_t), NEG)
    rank = jnp.zeros(shape, F32)
    for i in range(n_blocks):
        row = score[i:i + 1, :]
        beats = (row > score) | ((row == score) & (blk > i))
        rank = rank + jnp.where(beats, 1.0, 0.0)
    return jnp.where((rank < n_top) & visible, 1.0, 0.0)


def _compressed_kv(y):
    n = y.shape[0]
    return (y[:, :DH_B] + pltpu.roll(y[:, DH_B:], n - 1, 0)).astype(BF16)


def _diff_prompt_body(lamp_ref, g_ref, q_ref, k_ref, v_ref, o_ref, *, tq, lam_init):
    qi = pl.program_id(2)
    lam = _lambda(lamp_ref, lam_init)
    q = q_ref[...]
    lane = lax.broadcasted_iota(jnp.int32, (1, 2 * DH_A), 1)
    zero = jnp.zeros_like(q)
    qq = jnp.concatenate([jnp.where(lane < DH_A, q, zero), jnp.where(lane >= DH_A, q, zero)], axis=0)

    def chunk(j):
        start = pl.multiple_of(j * tq, tq)
        return k_ref[pl.ds(start, tq), :], v_ref[pl.ds(start, tq), :]

    def body(j, carry):
        k, v = chunk(j)
        return _online_update(_dot_nt(qq, k), v, *carry)

    carry = lax.fori_loop(0, qi, body, _softmax_init(2 * tq, 2 * DH_A))
    k, v = chunk(qi)
    s = _dot_nt(qq, k)
    r = lax.broadcasted_iota(jnp.int32, s.shape, 0) & (tq - 1)
    c = lax.broadcasted_iota(jnp.int32, s.shape, 1)
    m, l, acc = _online_update(jnp.where(c <= r, s, NEG), v, *carry)
    o = acc / l
    o_ref[...] = (_rms(o[:tq] - lam * o[tq:], g_ref[...]) * (1.0 - lam_init)).astype(o_ref.dtype)


def _diff_prompt(q, k, v, lam_p, subln_g, batch, seq, lam_init, tq=512):
    n, w = q.shape
    dv = 2 * DH_A
    nq = seq // tq
    return pl.pallas_call(
        functools.partial(_diff_prompt_body, tq=tq, lam_init=lam_init),
        grid=(batch, H_A, nq),
        in_specs=[pl.BlockSpec(lam_p.shape, lambda b, h, i: (0, 0)),
                  pl.BlockSpec((1, dv), lambda b, h, i: (0, 0)),
                  pl.BlockSpec((tq, dv), lambda b, h, i: (b * nq + i, h)),
                  pl.BlockSpec((seq, dv), lambda b, h, i: (b, h)),
                  pl.BlockSpec((seq, dv), lambda b, h, i: (b, h))],
        out_specs=pl.BlockSpec((tq, dv), lambda b, h, i: (b * nq + i, h)),
        out_shape=jax.ShapeDtypeStruct((n, w), BF16),
        compiler_params=_cparams(("parallel", "parallel", "arbitrary"), 48),
        name="diff_prompt",
    )(lam_p, subln_g, q, k, v)


def _cmp_const_body(pek_ref, pev_ref, wk_ref, wv_ref, o_ref):
    for a, (pe_ref, w_ref) in enumerate(((pek_ref, wk_ref), (pev_ref, wv_ref))):
        cw = _dot(pe_ref[...].astype(BF16), w_ref[...])
        o_ref[a] = jnp.broadcast_to(cw[0:1, :DH_B] + cw[1:2, DH_B:], (8, DH_B))


def _cmp_const(pek, pev, wk, wv):
    return pl.pallas_call(
        _cmp_const_body,
        out_shape=jax.ShapeDtypeStruct((2, 8, DH_B), F32),
        name="cmp_const",
    )(pek, pev, wk, wv)


def _compress_weights(pe, phi):
    half = CMP_BLOCK // 2
    w = jnp.concatenate([phi[:half].reshape(half * DH_B, DH_B), phi[half:].reshape(half * DH_B, DH_B)], axis=1)
    pe2 = jnp.zeros((8, half * DH_B), F32)
    pe2 = pe2.at[0].set(pe[:half].reshape(-1)).at[1].set(pe[half:].reshape(-1))
    return w.astype(BF16), pe2


def _compress_prompt_body(rk_ref, rv_ref, wk_ref, wv_ref, c_ref, yk_ref, yv_ref):
    half = CMP_BLOCK // 2
    for a, (r_ref, w_ref, y_ref) in enumerate(((rk_ref, wk_ref, yk_ref), (rv_ref, wv_ref, yv_ref))):
        parts = []
        for g in range(G_B):
            parts.append(jnp.concatenate(
                [r_ref[:, (l * G_B + g) * DH_B:(l * G_B + g + 1) * DH_B] for l in range(half)], axis=1))
        a_mat = jnp.concatenate(parts, axis=0).astype(BF16)
        y = _dot(a_mat, w_ref[...])
        y = jnp.concatenate([y[:, :DH_B] + c_ref[a, 0:1, :], y[:, DH_B:]], axis=1)
        n_chunk = y.shape[0] // G_B
        for g in range(G_B):
            y_ref[g] = y[g * n_chunk:(g + 1) * n_chunk]


def _compress_prompt(kc, vc, wk, wv, const, batch, seq):
    n_chunk = seq // CMP_STRIDE
    width = CMP_STRIDE * G_B * DH_B
    rk = kc.reshape(batch, n_chunk, width)
    rv = vc.reshape(batch, n_chunk, width)
    r_spec = pl.BlockSpec((None, n_chunk, width), lambda b: (b, 0, 0))
    w_spec = pl.BlockSpec(wk.shape, lambda b: (0, 0))
    y_spec = pl.BlockSpec((None, G_B, n_chunk, 2 * DH_B), lambda b: (b, 0, 0, 0))
    y_shape = jax.ShapeDtypeStruct((batch, G_B, n_chunk, 2 * DH_B), F32)
    return pl.pallas_call(
        _compress_prompt_body,
        grid=(batch,),
        in_specs=[r_spec, r_spec, w_spec, w_spec, pl.BlockSpec(const.shape, lambda b: (0, 0, 0))],
        out_specs=[y_spec, y_spec],
        out_shape=[y_shape, y_shape],
        compiler_params=_cparams(("parallel",), 40),
        name="compress_prompt",
    )(rk, rv, wk, wv, const)


def _nsa_prompt_body(q_ref, yk_ref, yv_ref, kv_ref, gate_ref, selmap_ref, o_ref, *, tq, n_sel, n_top):
    qi = pl.program_id(1)
    rows = HPG_B * tq
    gq = G_B * DH_B
    q0 = qi * tq
    qpos_col = q0 + lax.broadcasted_iota(jnp.int32, (tq, 1), 0)
    qpos_row = q0 + lax.broadcasted_iota(jnp.int32, (1, tq), 1)

    def stack_rows(x):
        return jnp.concatenate([x] * HPG_B, axis=0)

    gates = gate_ref[...]
    for g in range(G_B):
        qg = jnp.concatenate([q_ref[:, (g * HPG_B + h) * DH_B:(g * HPG_B + h + 1) * DH_B]
                              for h in range(HPG_B)], axis=0)

        kcmp = _compressed_kv(yk_ref[g])
        vcmp = _compressed_kv(yv_ref[g])
        n_cmp = kcmp.shape[0]
        s = _dot_nt(qg, kcmp)
        cmp_end = lax.broadcasted_iota(jnp.int32, (tq, n_cmp), 1) * CMP_STRIDE + (CMP_BLOCK - 1)
        p = _masked_softmax(s, stack_rows(jnp.where(cmp_end <= qpos_col, 1.0, 0.0)) > 0.5)
        o_c = _dot(p.astype(BF16), vcmp)
        p_sum = p[0:tq]
        for h in range(1, HPG_B):
            p_sum = p_sum + p[h * tq:(h + 1) * tq]
        imp_t = lax.dot_general(selmap_ref[...], p_sum, (((1,), (1,)), ((), ())),
                                precision=lax.Precision.HIGHEST, preferred_element_type=F32)
        sel_t = _select_blocks_t(imp_t, qpos_row, n_sel, n_top)
        sel = jnp.concatenate([sel_t, jnp.zeros((LANES - n_sel, tq), F32)], axis=0).T.astype(BF16)

        tk = 2 * tq
        blk_of_key = lax.shift_right_logical(lax.broadcasted_iota(jnp.int32, (LANES, tk), 1),
                                             int(math.log2(SEL_BLOCK)))
        blk_row = lax.broadcasted_iota(jnp.int32, (LANES, tk), 0)
        key_col_s = lax.broadcasted_iota(jnp.int32, (tq, tk), 1)

        def add_bias(s, bias):
            return (s.reshape(HPG_B, tq, s.shape[1]) + bias[None]).reshape(s.shape)

        def sel_step(j, carry):
            start = pl.multiple_of(j * tk, tk)
            k = kv_ref[pl.ds(start, tk), g * DH_B:(g + 1) * DH_B]
            v = kv_ref[pl.ds(start, tk), gq + g * DH_B:gq + (g + 1) * DH_B]
            expand = jnp.where(blk_row == blk_of_key + j * (tk // SEL_BLOCK), 1.0, 0.0).astype(BF16)
            ok = (_dot(sel, expand) > 0.5) & (key_col_s + j * tk <= qpos_col)
            return _online_update(add_bias(_dot_nt(qg, k), jnp.where(ok, 0.0, NEG)), v, *carry)

        n_sel_steps = lax.shift_right_logical(qi + 2, 1)
        _, l, acc = lax.fori_loop(0, n_sel_steps, sel_step, _softmax_init(rows, DH_B))
        o_s = acc / l

        span = tq + WINDOW
        w_start = pl.multiple_of(jnp.maximum(q0 - WINDOW, 0), tq)
        k = kv_ref[pl.ds(w_start, span), 2 * gq + g * DH_B:2 * gq + (g + 1) * DH_B]
        v = kv_ref[pl.ds(w_start, span), 3 * gq + g * DH_B:3 * gq + (g + 1) * DH_B]
        dist = qpos_col - (w_start + lax.broadcasted_iota(jnp.int32, (tq, span), 1))
        s = add_bias(_dot_nt(qg, k), jnp.where((dist >= 0) & (dist <= WINDOW), 0.0, NEG))
        p = jnp.exp2(s - jnp.max(s, axis=-1, keepdims=True))
        o_w = _dot(p.astype(BF16), v) / jnp.sum(p, axis=-1, keepdims=True)

        for h in range(HPG_B):
            head = g * HPG_B + h
            rs = slice(h * tq, (h + 1) * tq)
            o = (gates[:, 3 * head:3 * head + 1] * o_c[rs] + gates[:, 3 * head + 1:3 * head + 2] * o_s[rs]
                 + gates[:, 3 * head + 2:3 * head + 3] * o_w[rs])
            o_ref[:, head * DH_B:(head + 1) * DH_B] = o.astype(o_ref.dtype)


def _nsa_prompt(q, yk, yv, nkv, gates, selmap, batch, seq, tq=256):
    n, w = q.shape
    nq = seq // tq
    assert seq % (2 * tq) == 0 and WINDOW % tq == 0 and seq >= tq + WINDOW
    n_sel = -(-seq // SEL_BLOCK)
    n_chunk = yk.shape[2]
    y_spec = pl.BlockSpec((None, G_B, n_chunk, 2 * DH_B), lambda b, i: (b, 0, 0, 0))
    return pl.pallas_call(
        functools.partial(_nsa_prompt_body, tq=tq, n_sel=n_sel, n_top=min(SEL_TOP_N, n_sel)),
        grid=(batch, nq),
        in_specs=[pl.BlockSpec((tq, w), lambda b, i: (b * nq + i, 0)),
                  y_spec, y_spec,
                  pl.BlockSpec((seq, nkv.shape[1]), lambda b, i: (b, 0)),
                  pl.BlockSpec((tq, LANES), lambda b, i: (b * nq + i, 0)),
                  pl.BlockSpec(selmap.shape, lambda b, i: (0, 0))],
        out_specs=pl.BlockSpec((tq, w), lambda b, i: (b * nq + i, 0)),
        out_shape=jax.ShapeDtypeStruct((n, w), BF16),
        compiler_params=_cparams(("parallel", "arbitrary"), 48),
        name="nsa_prompt",
    )(q, yk, yv, nkv, gates, selmap)


def _sel_map(n_chunk, n_cmp, n_sel, n_sel_pad):
    i = np.arange(n_chunk)[:, None] * CMP_STRIDE
    j = np.arange(n_sel_pad)[None, :] * SEL_BLOCK
    m = (i < j + SEL_BLOCK) & (i + CMP_BLOCK > j)
    m &= (np.arange(n_chunk)[:, None] < n_cmp) & (np.arange(n_sel_pad)[None, :] < n_sel)
    return jnp.asarray(m.astype(np.float32))


def _outproj_body(oa_ref, ob_ref, wa_ref, wb_ref, x_ref, gt_ref, g_ref, o_ref):
    m = _dot(oa_ref[...], wa_ref[...]) + _dot(ob_ref[...], wb_ref[...])
    o_ref[...] = x_ref[...] + gt_ref[...] * _rms(m, g_ref[...])


def _outproj(o_a, o_b, w_a, w_b, x, mod, k_gate, rows_per_group, g, tm):
    n, d = x.shape
    row = lambda i: (i, 0)
    const = lambda i: (0, 0)
    return pl.pallas_call(
        _outproj_body,
        grid=(n // tm,),
        in_specs=[pl.BlockSpec((tm, o_a.shape[1]), row), pl.BlockSpec((tm, o_b.shape[1]), row),
                  pl.BlockSpec(w_a.shape, const), pl.BlockSpec(w_b.shape, const),
                  pl.BlockSpec((tm, d), row),
                  _mod_spec(mod, k_gate, tm, rows_per_group),
                  pl.BlockSpec((1, d), const)],
        out_specs=pl.BlockSpec((tm, d), row),
        out_shape=jax.ShapeDtypeStruct((n, d), F32),
        compiler_params=_cparams(("parallel",), 48),
        name="outproj",
    )(o_a, o_b, w_a, w_b, x, mod, g)


def _page_specs(rows, n_pages):
    return [pl.BlockSpec((None, rows, LANES), lambda b, j, pt, r=r: (pt[b, j * n_pages + r], 0, 0))
            for r in range(n_pages)]


def _diff_decode_body(pt_ref, lamp_ref, g_ref, q_ref, knew_ref, vnew_ref, *rest, n_pages, tq, lam_init):
    k_pages = rest[:n_pages]
    v_pages = rest[n_pages:2 * n_pages]
    o_ref, q_scr, bias_scr, m_scr, l_scr, acc_scr = rest[2 * n_pages:]
    j = pl.program_id(1)
    rows = H_A * 2 * tq
    n_keys = k_pages[0].shape[0]

    @pl.when(j == 0)
    def _():
        q = q_ref[...]
        lane = lax.broadcasted_iota(jnp.int32, (1, 2 * DH_A), 1)
        parts = []
        for h in range(H_A):
            qh = q[:, h * 2 * DH_A:(h + 1) * 2 * DH_A]
            parts += [jnp.where(lane < DH_A, qh, 0.0), jnp.where(lane >= DH_A, qh, 0.0)]
        q_scr[...] = jnp.concatenate(parts, axis=0).astype(BF16)
        head_of_row = lax.shift_right_logical(lax.broadcasted_iota(jnp.int32, (rows, n_keys), 0),
                                              int(math.log2(2 * tq)))
        head_of_key = lax.broadcasted_iota(jnp.int32, (rows, n_keys), 1) & (H_A - 1)
        bias_scr[...] = jnp.where(head_of_row == head_of_key, 0.0, NEG)
        m_scr[...] = jnp.full(m_scr.shape, NEG, F32)
        l_scr[...] = jnp.zeros(l_scr.shape, F32)
        acc_scr[...] = jnp.zeros(acc_scr.shape, F32)

    qq = q_scr[...]
    ss = [_dot_nt(qq, k_pages[r][...].astype(BF16)) + bias_scr[...] for r in range(n_pages)]
    vs = [v_pages[r][...].astype(BF16) for r in range(n_pages)]
    m_scr[...], l_scr[...], acc_scr[...] = _online_update_pages(ss, vs, m_scr[...], l_scr[...], acc_scr[...])

    @pl.when(j == pl.num_programs(1) - 1)
    def _():
        n_new = knew_ref.shape[0]
        r_i = lax.broadcasted_iota(jnp.int32, (rows, n_new), 0)
        c_i = lax.broadcasted_iota(jnp.int32, (rows, n_new), 1)
        ok = ((lax.shift_right_logical(r_i, int(math.log2(2 * tq))) == (c_i & (H_A - 1)))
              & (lax.shift_right_logical(c_i, int(math.log2(H_A))) <= (r_i & (tq - 1))))
        s = jnp.where(ok, _dot_nt(qq, knew_ref[...].astype(BF16)), NEG)
        _, l, acc = _online_update(s, vnew_ref[...].astype(BF16), m_scr[...], l_scr[...], acc_scr[...])
        o = acc / l
        lam = _lambda(lamp_ref, lam_init)
        for h in range(H_A):
            base = h * 2 * tq
            d = o[base:base + tq] - lam * o[base + tq:base + 2 * tq]
            o_ref[:, h * 2 * DH_A:(h + 1) * 2 * DH_A] = _rms(d, g_ref[...]) * (1.0 - lam_init)


def _diff_decode(page_table, lam_p, subln_g, q, k_new, v_new, cache_k, cache_v, lam_init):
    bs, tq, w = q.shape
    n_pages_total = page_table.shape[1]
    pc = PAGES_PER_STEP
    rows = H_A * 2 * tq
    dv = 2 * DH_A
    page_rows = cache_k.shape[1]
    per_b = lambda b, j, pt: (b, 0, 0)
    const = lambda b, j, pt: (0, 0)
    grid_spec = pltpu.PrefetchScalarGridSpec(
        num_scalar_prefetch=1,
        grid=(bs, n_pages_total // pc),
        in_specs=[pl.BlockSpec(lam_p.shape, const), pl.BlockSpec((1, dv), const),
                  pl.BlockSpec((None, tq, w), per_b),
                  pl.BlockSpec((None,) + k_new.shape[1:], per_b),
                  pl.BlockSpec((None,) + v_new.shape[1:], per_b)]
        + _page_specs(page_rows, pc) + _page_specs(page_rows, pc),
        out_specs=pl.BlockSpec((None, tq, w), per_b),
        scratch_shapes=[pltpu.VMEM((rows, dv), BF16), pltpu.VMEM((rows, page_rows), F32),
                        pltpu.VMEM((rows, 1), F32), pltpu.VMEM((rows, 1), F32), pltpu.VMEM((rows, dv), F32)])
    return pl.pallas_call(
        functools.partial(_diff_decode_body, n_pages=pc, tq=tq, lam_init=lam_init),
        grid_spec=grid_spec,
        out_shape=jax.ShapeDtypeStruct((bs, tq, w), F32),
        compiler_params=_cparams(("parallel", "arbitrary"), 58),
        name="diff_decode",
    )(page_table, lam_p, subln_g, q, k_new, v_new, *([cache_k] * pc), *([cache_v] * pc))


def _chunk_perm(page_tokens):
    n = page_tokens * G_B
    chunks = page_tokens // CMP_STRIDE
    out = np.arange(n)
    l, g, c = out // (G_B * chunks), (out // chunks) % G_B, out % chunks
    src = (c * CMP_STRIDE + l) * G_B + g
    perm = np.zeros((n, n), np.float32)
    perm[out, src] = 1.0
    return jnp.asarray(perm, BF16)


def _compress_decode_body(pt_ref, perm_ref, wk_ref, wv_ref, c_ref, *rest, n_pages):
    k_pages = rest[:n_pages]
    v_pages = rest[n_pages:2 * n_pages]
    yk_ref, yv_ref, a_scr = rest[2 * n_pages:]
    chunks = k_pages[0].shape[0] // (G_B * CMP_STRIDE)
    per_g = n_pages * chunks
    for a, (pages, w_ref, y_ref) in enumerate(((k_pages, wk_ref, yk_ref), (v_pages, wv_ref, yv_ref))):
        side_by_side = jnp.concatenate([pages[p][...].astype(BF16) for p in range(n_pages)], axis=1)
        pp = _dot(perm_ref[...], side_by_side)
        for l in range(CMP_STRIDE):
            for g in range(G_B):
                src = (l * G_B + g) * chunks
                for p in range(0, n_pages, 2):
                    dst = g * per_g + p * chunks
                    pair = jnp.concatenate([pp[src:src + chunks, p * DH_B:(p + 1) * DH_B],
                                            pp[src:src + chunks, (p + 1) * DH_B:(p + 2) * DH_B]], axis=0)
                    a_scr[dst:dst + 2 * chunks, l * DH_B:(l + 1) * DH_B] = pair.astype(BF16)
        y = _dot(a_scr[...], w_ref[...])
        y = jnp.concatenate([y[:, :DH_B] + c_ref[a, 0:1, :], y[:, DH_B:]], axis=1)
        for g in range(G_B):
            y_ref[g] = y[g * per_g:(g + 1) * per_g]


def _compress_decode(page_table, perm, wk, wv, const, cache_k, cache_v):
    bs, n_pages_total = page_table.shape
    pc = GROUP_PAGES_PER_STEP
    page_rows = cache_k.shape[1]
    chunks = page_rows // (G_B * CMP_STRIDE)
    n_chunk = n_pages_total * chunks
    const2 = lambda b, j, pt: (0, 0)
    y_spec = pl.BlockSpec((None, G_B, pc * chunks, 2 * DH_B), lambda b, j, pt: (b, 0, j, 0))
    y_shape = jax.ShapeDtypeStruct((bs, G_B, n_chunk, 2 * DH_B), F32)
    grid_spec = pltpu.PrefetchScalarGridSpec(
        num_scalar_prefetch=1,
        grid=(bs, n_pages_total // pc),
        in_specs=[pl.BlockSpec(perm.shape, const2), pl.BlockSpec(wk.shape, const2), pl.BlockSpec(wv.shape, const2),
                  pl.BlockSpec(const.shape, lambda b, j, pt: (0, 0, 0))]
        + _page_specs(page_rows, pc) + _page_specs(page_rows, pc),
        out_specs=[y_spec, y_spec],
        scratch_shapes=[pltpu.VMEM((G_B * pc * chunks, CMP_STRIDE * DH_B), BF16)])
    return pl.pallas_call(
        functools.partial(_compress_decode_body, n_pages=pc),
        grid_spec=grid_spec,
        out_shape=[y_shape, y_shape],
        compiler_params=_cparams(("parallel", "arbitrary"), 40),
        name="compress_decode",
    )(page_table, perm, wk, wv, const, *([cache_k] * pc), *([cache_v] * pc))


def _nsa_decode_body(pt_ref, q_ref, yk_ref, yv_ref, selmap_ref, expand_ref, gate_ref, ksn_ref, vsn_ref, kwn_ref,
                     vwn_ref, wk_ref, wv_ref, *rest, n_pages, tq, past_len, n_sel, n_top):
    k_pages = rest[:n_pages]
    v_pages = rest[n_pages:2 * n_pages]
    o_ref, wko_ref, wvo_ref, q_scr, sel_scr, bias_scr, oc_scr, m_scr, l_scr, acc_scr = rest[2 * n_pages:]
    j = pl.program_id(1)
    rows = H_B * tq
    rows_g = HPG_B * tq
    n_keys = k_pages[0].shape[0]
    log_tq = int(math.log2(tq))
    log_g = int(math.log2(G_B))

    def group_of_row(shape):
        return lax.shift_right_logical(lax.broadcasted_iota(jnp.int32, shape, 0), int(math.log2(rows_g)))

    def qpos_of_row(shape):
        return past_len + (lax.broadcasted_iota(jnp.int32, shape, 0) & (tq - 1))

    @pl.when(j == 0)
    def _():
        q = q_ref[...]
        q_scr[...] = jnp.concatenate([q[:, h * DH_B:(h + 1) * DH_B] for h in range(H_B)], axis=0).astype(BF16)
        qpos_col = past_len + lax.broadcasted_iota(jnp.int32, (tq, 1), 0)
        for g in range(G_B):
            qg = q_scr[g * rows_g:(g + 1) * rows_g, :]
            kcmp = _compressed_kv(yk_ref[g])
            vcmp = _compressed_kv(yv_ref[g])
            n_cmp = kcmp.shape[0]
            s = _dot_nt(qg, kcmp)
            cmp_end = lax.broadcasted_iota(jnp.int32, (rows_g, n_cmp), 1) * CMP_STRIDE + (CMP_BLOCK - 1)
            p = _masked_softmax(s, cmp_end <= qpos_of_row((rows_g, n_cmp)))
            oc_scr[g * rows_g:(g + 1) * rows_g, :] = _dot(p.astype(BF16), vcmp)
            p_sum = p[0:tq]
            for h in range(1, HPG_B):
                p_sum = p_sum + p[h * tq:(h + 1) * tq]
            imp = jnp.dot(p_sum, selmap_ref[...], precision=lax.Precision.HIGHEST, preferred_element_type=F32)
            sel = _select_blocks(imp, qpos_col, n_sel, n_top)
            for h in range(HPG_B):
                sel_scr[g * rows_g + h * tq:g * rows_g + (h + 1) * tq, :] = sel
        sel_rows = sel_scr[...]
        n_lanes = sel_rows.shape[1]
        exp_slots = expand_ref.shape[1]
        blocks_per_exp = exp_slots // (SEL_BLOCK * G_B)
        slot = lax.broadcasted_iota(jnp.int32, (rows, exp_slots), 1)
        same_group = group_of_row((rows, exp_slots)) == (slot & (G_B - 1))
        for c in range(bias_scr.shape[1] // exp_slots):
            shifted = pltpu.roll(sel_rows, (n_lanes - c * blocks_per_exp) % n_lanes, 1) if c else sel_rows
            chosen = _dot(shifted[:, :LANES].astype(BF16), expand_ref[...])
            bias_scr[:, c * exp_slots:(c + 1) * exp_slots] = jnp.where((chosen > 0.5) & same_group, 0.0, NEG)
        m_scr[...] = jnp.full(m_scr.shape, NEG, F32)
        l_scr[...] = jnp.zeros(l_scr.shape, F32)
        acc_scr[...] = jnp.zeros(acc_scr.shape, F32)

    qq = q_scr[...]
    step_slots = n_pages * n_keys
    bias = bias_scr[:, pl.ds(pl.multiple_of(j * step_slots, step_slots), step_slots)]
    ss = [_dot_nt(qq, k_pages[r][...].astype(BF16)) + bias[:, r * n_keys:(r + 1) * n_keys] for r in range(n_pages)]
    vs = [v_pages[r][...].astype(BF16) for r in range(n_pages)]
    m_scr[...], l_scr[...], acc_scr[...] = _online_update_pages(ss, vs, m_scr[...], l_scr[...], acc_scr[...])

    @pl.when(j == pl.num_programs(1) - 1)
    def _():
        sel_rows = sel_scr[...]
        sel_lane = lax.broadcasted_iota(jnp.int32, sel_rows.shape, 1)

        def block_selected(blk):
            return jnp.sum(jnp.where(sel_lane == blk, sel_rows, 0.0), axis=1, keepdims=True) > 0.5

        n_new = ksn_ref.shape[0]
        c_new = lax.broadcasted_iota(jnp.int32, (rows, n_new), 1)
        t_new = lax.shift_right_logical(c_new, log_g)
        grp_ok = group_of_row((rows, n_new)) == (c_new & (G_B - 1))
        causal = (past_len + t_new) <= qpos_of_row((rows, n_new))
        ok = grp_ok & causal & block_selected(past_len // SEL_BLOCK)
        s = jnp.where(ok, _dot_nt(qq, ksn_ref[...].astype(BF16)), NEG)
        _, l, acc = _online_update(s, vsn_ref[...].astype(BF16), m_scr[...], l_scr[...], acc_scr[...])
        o_s = acc / l

        n_win = wk_ref.shape[0]
        win_tokens = n_win // G_B
        c_w = lax.broadcasted_iota(jnp.int32, (rows, n_win), 1)
        kpos = past_len - win_tokens + lax.shift_right_logical(c_w, log_g)
        dist = qpos_of_row((rows, n_win)) - kpos
        ok_w = (group_of_row((rows, n_win)) == (c_w & (G_B - 1))) & (dist >= 0) & (dist <= WINDOW) & (kpos >= 0)
        s_w = jnp.where(ok_w, _dot_nt(qq, wk_ref[...].astype(BF16)), NEG)
        dist_n = qpos_of_row((rows, n_new)) - (past_len + t_new)
        ok_n = grp_ok & (dist_n >= 0) & (dist_n <= WINDOW)
        s_n = jnp.where(ok_n, _dot_nt(qq, kwn_ref[...].astype(BF16)), NEG)
        m = jnp.maximum(jnp.max(s_w, axis=-1, keepdims=True), jnp.max(s_n, axis=-1, keepdims=True))
        e_w = jnp.where(ok_w, jnp.exp2(s_w - m), 0.0)
        e_n = jnp.where(ok_n, jnp.exp2(s_n - m), 0.0)
        l_w = jnp.sum(e_w, axis=-1, keepdims=True) + jnp.sum(e_n, axis=-1, keepdims=True)
        o_w = (_dot(e_w.astype(BF16), wv_ref[...].astype(BF16))
               + _dot(e_n.astype(BF16), vwn_ref[...].astype(BF16))) / jnp.where(l_w > 0.0, l_w, 1.0)

        gates = gate_ref[...]
        o_c = oc_scr[...]
        for head in range(H_B):
            rs = slice(head * tq, (head + 1) * tq)
            o_ref[:, head * DH_B:(head + 1) * DH_B] = (
                gates[:, 3 * head:3 * head + 1] * o_c[rs] + gates[:, 3 * head + 1:3 * head + 2] * o_s[rs]
                + gates[:, 3 * head + 2:3 * head + 3] * o_w[rs])

        keep = n_win - n_new
        wko_ref[0:keep, :] = wk_ref[n_new:n_win, :]
        wko_ref[keep:n_win, :] = kwn_ref[...]
        wvo_ref[0:keep, :] = wv_ref[n_new:n_win, :]
        wvo_ref[keep:n_win, :] = vwn_ref[...]


def _nsa_decode(page_table, q, yk, yv, selmap, gates, ks_new, vs_new, kw_new, vw_new, win_k, win_v,
                cache_k, cache_v, past_len, n_sel):
    bs, tq, w = q.shape
    n_pages_total = page_table.shape[1]
    pc = GROUP_PAGES_PER_STEP
    rows = H_B * tq
    page_rows = cache_k.shape[1]
    exp_slots = 2048
    slots_per_block = SEL_BLOCK * G_B
    expand = jnp.asarray(np.arange(LANES)[:, None] == (np.arange(exp_slots)[None, :] // slots_per_block), BF16)
    assert (n_pages_total * page_rows) % exp_slots == 0 and exp_slots // slots_per_block <= LANES
    per_b3 = lambda b, j, pt: (b, 0, 0)
    per_b4 = lambda b, j, pt: (b, 0, 0, 0)
    const2 = lambda b, j, pt: (0, 0)
    blk3 = lambda a: pl.BlockSpec((None,) + a.shape[1:], per_b3)
    y_spec = pl.BlockSpec((None,) + yk.shape[1:], per_b4)
    grid_spec = pltpu.PrefetchScalarGridSpec(
        num_scalar_prefetch=1,
        grid=(bs, n_pages_total // pc),
        in_specs=[blk3(q), y_spec, y_spec, pl.BlockSpec(selmap.shape, const2), pl.BlockSpec(expand.shape, const2),
                  blk3(gates), blk3(ks_new), blk3(vs_new), blk3(kw_new), blk3(vw_new), blk3(win_k), blk3(win_v)]
        + _page_specs(page_rows, pc) + _page_specs(page_rows, pc),
        out_specs=[blk3(q), blk3(win_k), blk3(win_v)],
        scratch_shapes=[pltpu.VMEM((rows, DH_B), BF16), pltpu.VMEM((rows, selmap.shape[1]), F32),
                        pltpu.VMEM((rows, n_pages_total * page_rows), F32),
                        pltpu.VMEM((rows, DH_B), F32), pltpu.VMEM((rows, 1), F32), pltpu.VMEM((rows, 1), F32),
                        pltpu.VMEM((rows, DH_B), F32)])
    return pl.pallas_call(
        functools.partial(_nsa_decode_body, n_pages=pc, tq=tq, past_len=past_len, n_sel=n_sel,
                          n_top=min(SEL_TOP_N, n_sel)),
        grid_spec=grid_spec,
        out_shape=[jax.ShapeDtypeStruct((bs, tq, w), F32), jax.ShapeDtypeStruct(win_k.shape, F32),
                   jax.ShapeDtypeStruct(win_v.shape, F32)],
        compiler_params=_cparams(("parallel", "arbitrary"), 48),
        name="nsa_decode",
    )(page_table, q, yk, yv, selmap, expand, gates, ks_new, vs_new, kw_new, vw_new, win_k, win_v,
      *([cache_k] * pc), *([cache_v] * pc))


def kernel(x_prompt, x_sample, cache_diff_k, cache_diff_v, cache_cmp_k, cache_cmp_v, cache_sel_k, cache_sel_v,
           cache_win_k, cache_win_v, page_table, c_prompt, c_sample, w_mod, b_mod, norm_g, w_in, w_out, lam_p,
           subln_g, cmp_pe, cmp_phi, ffn_gate, ffn_up, ffn_down):
    depth = w_mod.shape[0]
    assert depth == 1, "single-layer step"
    B, T, D = x_prompt.shape
    Bs, Ts, _ = x_sample.shape
    n_pool, page = cache_diff_k.shape[1], cache_diff_k.shape[2]
    past_len = page_table.shape[1] * page
    win_len = cache_win_k.shape[2]
    assert T % 512 == 0 and win_len <= T and win_len == WINDOW and past_len >= win_len
    assert past_len % SEL_BLOCK == 0 and Ts <= SEL_BLOCK and (Ts & (Ts - 1)) == 0
    assert (past_len + Ts - CMP_BLOCK) // CMP_STRIDE + 1 <= past_len // CMP_STRIDE
    lam_init = 0.8 - 0.6 * math.exp(-0.3 * 0)
    Np, Ns = B * T, Bs * Ts
    gq = G_B * DH_B

    wg, wu, wd = ffn_gate[0].astype(BF16), ffn_up[0].astype(BF16), ffn_down[0].astype(BF16)
    w_pad = jnp.pad(w_in[0].astype(BF16), ((0, 0), (0, N_IN_BLOCKS * IN_BLOCK - w_in.shape[2])))
    w_out_a, w_out_b = w_out[0, :H_A * 2 * DH_A].astype(BF16), w_out[0, H_A * 2 * DH_A:].astype(BF16)
    ng = norm_g[0].reshape(6, 1, D)
    wck, pek = _compress_weights(cmp_pe[0, 0], cmp_phi[0, 0])
    wcv, pev = _compress_weights(cmp_pe[0, 1], cmp_phi[0, 1])
    cmp_const = _cmp_const(pek, pev, wck, wcv)

    n_c = B + Bs
    c_all = jnp.pad(jnp.concatenate([c_prompt, c_sample], axis=0), ((0, -n_c % 8), (0, 0)))
    mod = _adaln_mod(c_all, w_mod[0], b_mod[0][None]).reshape(c_all.shape[0], 9, D)
    mod_p = mod[:B].transpose(1, 0, 2).reshape(9, B, 1, D)
    mod_s = jnp.repeat(mod[B:n_c].transpose(1, 0, 2), Ts, axis=1)

    def split_states(pr, batch, seq):
        return (pr["k_a"].reshape(1, batch, seq, H_A, 2 * DH_A), pr["v_a"].reshape(1, batch, seq, H_A, 2 * DH_A),
                pr["kc"].reshape(1, batch, seq, G_B, DH_B), pr["vc"].reshape(1, batch, seq, G_B, DH_B),
                pr["ks"].reshape(1, batch, seq, G_B, DH_B), pr["vs"].reshape(1, batch, seq, G_B, DH_B))

    names = ("q_a", "k_a", "k_a_bf", "v_a", "v_a_bf", "q_b", "kc", "vc", "ks", "vs", "kw", "vw", "nkv", "gates")

    tm_p = 512
    xp = x_prompt.reshape(Np, D)
    x1 = _ffn(xp, mod_p, (0, 1, 2), T, ng[0], ng[1], wg, wu, wd, 0, tm_p, FFN_TF)
    pr = dict(zip(names, _inproj(x1, mod_p, 3, 4, T, ng[2], w_pad, _rope_tables(np.arange(T)), tm_p)))
    o_a = _diff_prompt(pr["q_a"], pr["k_a_bf"], pr["v_a_bf"], lam_p[0], subln_g[0][None], B, T, lam_init)
    yk, yv = _compress_prompt(pr["kc"], pr["vc"], wck, wcv, cmp_const, B, T)
    n_cmp_p = (T - CMP_BLOCK) // CMP_STRIDE + 1
    n_sel_p = -(-T // SEL_BLOCK)
    assert n_sel_p <= LANES
    selmap_p = _sel_map(T // CMP_STRIDE, n_cmp_p, n_sel_p, n_sel_p).T
    o_b = _nsa_prompt(pr["q_b"], yk, yv, pr["nkv"], pr["gates"], selmap_p, B, T)
    x2 = _outproj(o_a, o_b, w_out_a, w_out_b, x1, mod_p, 5, T, ng[3], tm_p)
    y_p = _ffn(x2, mod_p, (6, 7, 8), T, ng[4], ng[5], wg, wu, wd, 1, tm_p, FFN_TF).reshape(B, T, D)
    st_p = split_states(pr, B, T)
    wk_p = pr["kw"].reshape(1, B, T, G_B, DH_B)[:, :, T - win_len:]
    wv_p = pr["vw"].reshape(1, B, T, G_B, DH_B)[:, :, T - win_len:]

    xs = x_sample.reshape(Ns, D)
    x1s = _ffn(xs, mod_s, (0, 1, 2), 1, ng[0], ng[1], wg, wu, wd, 0, Ns, FFN_TF)
    pos_s = np.tile(past_len + np.arange(Ts), Bs)
    ps = dict(zip(names, _inproj(x1s, mod_s, 3, 4, 1, ng[2], w_pad, _rope_tables(pos_s), Ns)))
    ck = cache_diff_k[0].reshape(n_pool, page * H_A, 2 * DH_A)
    cv = cache_diff_v[0].reshape(n_pool, page * H_A, 2 * DH_A)
    o_as = _diff_decode(page_table, lam_p[0], subln_g[0][None],
                        ps["q_a"].astype(F32).reshape(Bs, Ts, H_A * 2 * DH_A),
                        ps["k_a"].reshape(Bs, Ts * H_A, 2 * DH_A), ps["v_a"].reshape(Bs, Ts * H_A, 2 * DH_A),
                        ck, cv, lam_init)
    pool3 = lambda c: c[0].reshape(n_pool, page * G_B, DH_B)
    yks, yvs = _compress_decode(page_table, _chunk_perm(page), wck, wcv, cmp_const,
                                pool3(cache_cmp_k), pool3(cache_cmp_v))
    tk_s = past_len + Ts
    n_cmp_s = (tk_s - CMP_BLOCK) // CMP_STRIDE + 1
    n_sel_s = -(-tk_s // SEL_BLOCK)
    selmap_s = _sel_map(past_len // CMP_STRIDE, n_cmp_s, n_sel_s, -(-n_sel_s // LANES) * LANES)
    new3 = lambda a: a.reshape(Bs, Ts * G_B, DH_B)
    o_bs, wk_s, wv_s = _nsa_decode(
        page_table, ps["q_b"].astype(F32).reshape(Bs, Ts, H_B * DH_B), yks, yvs, selmap_s,
        ps["gates"].reshape(Bs, Ts, LANES), new3(ps["ks"]), new3(ps["vs"]), new3(ps["kw"]), new3(ps["vw"]),
        cache_win_k[0].reshape(Bs, win_len * G_B, DH_B), cache_win_v[0].reshape(Bs, win_len * G_B, DH_B),
        pool3(cache_sel_k), pool3(cache_sel_v), past_len, n_sel_s)
    x2s = _outproj(o_as.reshape(Ns, -1).astype(BF16), o_bs.reshape(Ns, -1).astype(BF16), w_out_a, w_out_b,
                   x1s, mod_s, 5, 1, ng[3], Ns)
    y_s = _ffn(x2s, mod_s, (6, 7, 8), 1, ng[4], ng[5], wg, wu, wd, 1, Ns, FFN_TF).reshape(Bs, Ts, D)
    st_s = split_states(ps, Bs, Ts)
    wk_s = wk_s.reshape(1, Bs, win_len, G_B, DH_B)
    wv_s = wv_s.reshape(1, Bs, win_len, G_B, DH_B)

    return (y_p, y_s) + st_p + (wk_p, wv_p) + st_s + (wk_s, wv_s)
```
